```python
import jax, jax.numpy as jnp
from jax import lax
import numpy as np

D_MODEL = 1024
BATCH = 8
SEQ = 8192
DEPTH = 1
DEC_BATCH = 128
DEC_SEQ = 4
PAST_LEN = 8192
PAGE_SIZE = 128

D_LRU = D_MODEL
LRU_BLOCKS = 8
LRU_BLOCK_W = D_LRU // LRU_BLOCKS
CONV_W = 4
LRU_C = 8.0
N_HEADS = 8
HEAD_DIM = D_MODEL // N_HEADS
D_ATT = N_HEADS * HEAD_DIM
Q_BLOCK = 128
SB_SCALE = HEAD_DIM ** -0.5
SB_BIAS_INIT = -6.0
SPLITS = [D_LRU, D_LRU + D_ATT, D_LRU + 2 * D_ATT, D_LRU + 3 * D_ATT, D_LRU + 3 * D_ATT + D_MODEL]
W_IN_COLS = D_LRU + 3 * D_ATT + 2 * D_MODEL
N_KEYS = 128
N_EXPERTS = N_KEYS * N_KEYS
PEER_HEADS = 8
PEER_TOPK = 16
D_QUERY = 256
D_HALF = D_QUERY // 2
PEER_CHUNK = 128
EPS = 1e-6

kernel_name = 'hawk_stickbreak_peer_adaln_step'


def rmsnorm(x, g):
    xf = x.astype(jnp.float32)
    y = xf * lax.rsqrt(jnp.mean(xf * xf, axis=-1, keepdims=True) + EPS)
    return (y * g.astype(jnp.float32)).astype(x.dtype)


def rglru_branch(xl, conv_prev, h0, conv_w, conv_b, wa, ba, wx, bx, lam):
    n, t, _ = xl.shape
    xpad = jnp.concatenate([conv_prev.astype(xl.dtype), xl], axis=1)
    xc = conv_b + sum(conv_w[i] * xpad[:, i:i + t] for i in range(CONV_W))
    new_conv = xpad[:, t:]
    xb = xc.reshape(n, t, LRU_BLOCKS, LRU_BLOCK_W)
    r = jax.nn.sigmoid((jnp.einsum('ntgi,gij->ntgj', xb, wa).reshape(n, t, D_LRU) + ba).astype(jnp.float32))
    ig = jax.nn.sigmoid((jnp.einsum('ntgi,gij->ntgj', xb, wx).reshape(n, t, D_LRU) + bx).astype(jnp.float32))
    log_a = -LRU_C * r * jax.nn.softplus(-lam.astype(jnp.float32))
    a = jnp.exp(log_a)
    b = jnp.sqrt(-jnp.expm1(2.0 * log_a)) * ig * xc.astype(jnp.float32)

    def step(h, ab):
        h = ab[0] * h + ab[1]
        return h, h

    h_last, hs = lax.scan(step, h0.astype(jnp.float32), (a.swapaxes(0, 1), b.swapaxes(0, 1)))
    return hs.swapaxes(0, 1).astype(xl.dtype), new_conv, h_last.astype(xl.dtype)


def sb_block(q, k, v, valid, carry, bias):
    acc, out = carry
    z = (jnp.einsum('nqhd,nkhd->nhqk', q, k, preferred_element_type=jnp.float32) * SB_SCALE
         + bias.astype(jnp.float32)[None, :, None, None])
    log_beta = jax.nn.log_sigmoid(z)
    log_keep = jnp.where(valid, jax.nn.log_sigmoid(-z), 0.0)
    later = lax.cumsum(log_keep, axis=3, reverse=True) - log_keep + acc[..., None]
    w = jnp.where(valid, jnp.exp(log_beta + later), 0.0)
    out = out + jnp.einsum('nhqk,nkhd->nhqd', w, v.astype(jnp.float32))
    return acc + log_keep.sum(axis=3), out


def sb_attention_prompt(q, k, v, bias):
    n, s = q.shape[:2]
    nb = s // Q_BLOCK
    blk = jnp.arange(Q_BLOCK)

    def one_query_block(i):
        q_i = lax.dynamic_slice_in_dim(q, i * Q_BLOCK, Q_BLOCK, axis=1)
        t_pos = i * Q_BLOCK + blk

        def body(m, carry):
            j = i - m
            k_j = lax.dynamic_slice_in_dim(k, j * Q_BLOCK, Q_BLOCK, axis=1)
            v_j = lax.dynamic_slice_in_dim(v, j * Q_BLOCK, Q_BLOCK, axis=1)
            valid = (j * Q_BLOCK + blk)[None, :] < t_pos[:, None]
            return sb_block(q_i, k_j, v_j, valid, carry, bias)

        init = (jnp.zeros((n, N_HEADS, Q_BLOCK), jnp.float32),
                jnp.zeros((n, N_HEADS, Q_BLOCK, HEAD_DIM), jnp.float32))
        _, out = lax.fori_loop(0, i + 1, body, init)
        return out.transpose(0, 2, 1, 3).reshape(n, Q_BLOCK, D_ATT)

    outs = lax.map(one_query_block, jnp.arange(nb))
    return outs.transpose(1, 0, 2, 3).reshape(n, s, D_ATT).astype(q.dtype)


def sb_attention_sample(q, k_new, v_new, bias, cache_k, cache_v, page_table, layer):
    n, t = q.shape[:2]
    pos = jnp.arange(t)
    carry = (jnp.zeros((n, N_HEADS, t), jnp.float32), jnp.zeros((n, N_HEADS, t, HEAD_DIM), jnp.float32))
    carry = sb_block(q, k_new, v_new, pos[None, :] < pos[:, None], carry, bias)
    full = jnp.ones((t, PAGE_SIZE), bool)

    def page_step(carry, phys):
        return sb_block(q, cache_k[layer, phys], cache_v[layer, phys], full, carry, bias), None

    (_, out), _ = lax.scan(page_step, carry, page_table[:, ::-1].T)
    return out.transpose(0, 2, 1, 3).reshape(n, t, D_ATT).astype(q.dtype)


def peer_ffn(h, w_q, sub_k1, sub_k2, u_tab, v_tab):
    n, t, d = h.shape
    ht = h.reshape(n * t, d)
    ntok = n * t
    pad = (-ntok) % PEER_CHUNK
    chunks = jnp.pad(ht, ((0, pad), (0, 0))).reshape(-1, PEER_CHUNK, d)

    def one(hc):
        qh = (hc @ w_q).reshape(PEER_CHUNK, PEER_HEADS, 2, D_HALF).astype(jnp.float32)
        s1 = jnp.einsum('chd,hkd->chk', qh[:, :, 0], sub_k1.astype(jnp.float32))
        s2 = jnp.einsum('chd,hkd->chk', qh[:, :, 1], sub_k2.astype(jnp.float32))
        v1, i1 = lax.top_k(s1, PEER_TOPK)
        v2, i2 = lax.top_k(s2, PEER_TOPK)
        cand = (v1[..., :, None] + v2[..., None, :]).reshape(PEER_CHUNK, PEER_HEADS, PEER_TOPK * PEER_TOPK)
        cidx = (i1[..., :, None] * N_KEYS + i2[..., None, :]).reshape(PEER_CHUNK, PEER_HEADS, PEER_TOPK * PEER_TOPK)
        best, sel = lax.top_k(cand, PEER_TOPK)
        eidx = jnp.take_along_axis(cidx, sel, axis=-1)
        g = jax.nn.softmax(best, axis=-1)
        u = u_tab[eidx]
        act = jax.nn.gelu(jnp.einsum('cd,chkd->chk', hc, u).astype(jnp.float32))
        return jnp.einsum('chk,chkd->cd', (g * act).astype(hc.dtype), v_tab[eidx])

    out = lax.map(one, chunks).reshape(-1, d)[:ntok]
    return out.reshape(n, t, d)


def trunk_layer(x, c, attend, conv_prev, h0, p):
    n, t = x.shape[:2]
    mod = jax.nn.silu(c) @ p['w_ada'] + p['b_ada']
    shift1, scale1, gate1, shift2, scale2, gate2 = [m[:, None, :] for m in jnp.split(mod, 6, axis=-1)]
    h = rmsnorm(x, p['g_mix']) * (1 + scale1) + shift1
    proj = h @ p['w_in']
    xl, q, k, v, gl, ga = jnp.split(proj, SPLITS, axis=-1)
    q = q.reshape(n, t, N_HEADS, HEAD_DIM)
    k = k.reshape(n, t, N_HEADS, HEAD_DIM)
    v = v.reshape(n, t, N_HEADS, HEAD_DIM)
    yl, conv_new, h_new = rglru_branch(xl, conv_prev, h0, p['conv_w'], p['conv_b'], p['lru_wa'],
                                       p['lru_ba'], p['lru_wx'], p['lru_bx'], p['lru_lambda'])
    ya = attend(q, k, v, p['sb_bias'])
    merged = jax.nn.sigmoid(gl) * (yl @ p['w_lru_out']) + jax.nn.sigmoid(ga) * (ya @ p['w_att_out'])
    x = x + gate1 * (merged @ p['w_o'])
    h2 = rmsnorm(x, p['g_ffn']) * (1 + scale2) + shift2
    x = x + gate2 * peer_ffn(h2, p['peer_wq'], p['peer_k1'], p['peer_k2'], p['peer_u'], p['peer_v'])
    return x, k, v, conv_new, h_new


def setup_inputs(seed: int = 0) -> dict:
    key = jax.random.key(seed)
    ks = jax.random.split(key, 32)
    f32 = jnp.float32

    def nrm(k, shape, s):
        return jax.random.normal(k, shape, f32) * s

    n_pages = PAST_LEN // PAGE_SIZE
    n_phys = (DEC_BATCH * n_pages * 5 + 3) // 4
    page_table = jax.random.permutation(ks[0], n_phys)[:DEC_BATCH * n_pages].reshape(DEC_BATCH, n_pages).astype(jnp.int32)
    a8 = jax.random.uniform(ks[1], (DEPTH, D_LRU), f32, 0.9, 0.999)
    a = a8 ** (1.0 / LRU_C)
    lru_lambda = jnp.log(a) - jnp.log1p(-a)
    return {
        'x_prompt': nrm(ks[2], (BATCH, SEQ, D_MODEL), 1.0),
        'x_sample': nrm(ks[3], (DEC_BATCH, DEC_SEQ, D_MODEL), 1.0),
        'cache_k': nrm(ks[4], (DEPTH, n_phys, PAGE_SIZE, N_HEADS, HEAD_DIM), 1.0),
        'cache_v': nrm(ks[5], (DEPTH, n_phys, PAGE_SIZE, N_HEADS, HEAD_DIM), 1.0),
        'state_conv': nrm(ks[6], (DEPTH, DEC_BATCH, CONV_W - 1, D_LRU), 1.0),
        'state_lru': nrm(ks[7], (DEPTH, DEC_BATCH, D_LRU), 1.0),
        'page_table': page_table,
        'c_prompt': nrm(ks[8], (BATCH, D_MODEL), 1.0),
        'c_sample': nrm(ks[9], (DEC_BATCH, D_MODEL), 1.0),
        'g_mix': 1.0 + nrm(ks[10], (DEPTH, D_MODEL), 0.02),
        'w_ada': nrm(ks[11], (DEPTH, D_MODEL, 6 * D_MODEL), D_MODEL ** -0.5),
        'b_ada': nrm(ks[12], (DEPTH, 6 * D_MODEL), 0.02),
        'w_in': nrm(ks[13], (DEPTH, D_MODEL, W_IN_COLS), D_MODEL ** -0.5),
        'conv_w': nrm(ks[14], (DEPTH, CONV_W, D_LRU), CONV_W ** -0.5),
        'conv_b': nrm(ks[15], (DEPTH, D_LRU), 0.02),
        'lru_wa': nrm(ks[16], (DEPTH, LRU_BLOCKS, LRU_BLOCK_W, LRU_BLOCK_W), LRU_BLOCK_W ** -0.5),
        'lru_ba': nrm(ks[17], (DEPTH, D_LRU), 0.02),
        'lru_wx': nrm(ks[18], (DEPTH, LRU_BLOCKS, LRU_BLOCK_W, LRU_BLOCK_W), LRU_BLOCK_W ** -0.5),
        'lru_bx': nrm(ks[19], (DEPTH, D_LRU), 0.02),
        'lru_lambda': lru_lambda,
        'sb_bias': SB_BIAS_INIT + nrm(ks[30], (DEPTH, N_HEADS), 0.3),
        'w_lru_out': nrm(ks[20], (DEPTH, D_LRU, D_MODEL), D_LRU ** -0.5),
        'w_att_out': nrm(ks[21], (DEPTH, D_ATT, D_MODEL), D_ATT ** -0.5),
        'w_o': nrm(ks[22], (DEPTH, D_MODEL, D_MODEL), D_MODEL ** -0.5),
        'g_ffn': 1.0 + nrm(ks[23], (DEPTH, D_MODEL), 0.02),
        'peer_wq': nrm(ks[24], (DEPTH, D_MODEL, PEER_HEADS * D_QUERY), D_MODEL ** -0.5),
        'peer_k1': nrm(ks[25], (DEPTH, PEER_HEADS, N_KEYS, D_HALF), D_HALF ** -0.5),
        'peer_k2': nrm(ks[26], (DEPTH, PEER_HEADS, N_KEYS, D_HALF), D_HALF ** -0.5),
        'peer_u': nrm(ks[27], (DEPTH, N_EXPERTS, D_MODEL), D_MODEL ** -0.5),
        'peer_v': nrm(ks[28], (DEPTH, N_EXPERTS, D_MODEL), 0.5),
        'g_final': 1.0 + nrm(ks[29], (D_MODEL,), 0.02),
    }


def reference(x_prompt, x_sample, cache_k, cache_v, state_conv, state_lru, page_table, c_prompt, c_sample,
              g_mix, w_ada, b_ada, w_in, conv_w, conv_b, lru_wa, lru_ba, lru_wx, lru_bx, lru_lambda, sb_bias,
              w_lru_out, w_att_out, w_o, g_ffn, peer_wq, peer_k1, peer_k2, peer_u, peer_v, g_final):
    xp, xs = x_prompt, x_sample
    kp_l, vp_l, cp_l, hp_l, ks_l, vs_l, cs_l, hs_l = [], [], [], [], [], [], [], []
    for l in range(DEPTH):
        p = {'g_mix': g_mix[l], 'w_ada': w_ada[l], 'b_ada': b_ada[l], 'w_in': w_in[l],
             'conv_w': conv_w[l], 'conv_b': conv_b[l], 'lru_wa': lru_wa[l], 'lru_ba': lru_ba[l],
             'lru_wx': lru_wx[l], 'lru_bx': lru_bx[l], 'lru_lambda': lru_lambda[l], 'sb_bias': sb_bias[l],
             'w_lru_out': w_lru_out[l], 'w_att_out': w_att_out[l], 'w_o': w_o[l], 'g_ffn': g_ffn[l],
             'peer_wq': peer_wq[l], 'peer_k1': peer_k1[l], 'peer_k2': peer_k2[l],
             'peer_u': peer_u[l], 'peer_v': peer_v[l]}
        nb = xp.shape[0]
        conv0 = jnp.zeros((nb, CONV_W - 1, D_LRU), xp.dtype)
        h0 = jnp.zeros((nb, D_LRU), xp.dtype)
        xp, kp, vp, cp, hp = trunk_layer(xp, c_prompt, sb_attention_prompt, conv0, h0, p)
        attend_sample = lambda q, k, v, b, l=l: sb_attention_sample(q, k, v, b, cache_k, cache_v, page_table, l)
        xs, kss, vss, css, hss = trunk_layer(xs, c_sample, attend_sample, state_conv[l], state_lru[l], p)
        kp_l.append(kp); vp_l.append(vp); cp_l.append(cp); hp_l.append(hp)
        ks_l.append(kss); vs_l.append(vss); cs_l.append(css); hs_l.append(hss)
    y_prompt = rmsnorm(xp, g_final)
    y_sample = rmsnorm(xs, g_final)
    return (y_prompt, y_sample,
            jnp.stack(kp_l), jnp.stack(vp_l), jnp.stack(cp_l), jnp.stack(hp_l),
            jnp.stack(ks_l), jnp.stack(vs_l), jnp.stack(cs_l), jnp.stack(hs_l))
```

```python
import functools

import numpy as np
import jax
import jax.numpy as jnp
from jax import lax
from jax.experimental import pallas as pl
from jax.experimental.pallas import tpu as pltpu

F32 = jnp.float32
BF16 = jnp.bfloat16
I32 = jnp.int32

LANES = 128
SUBLANES = 8
D_MODEL = 1024
N_CHUNK = D_MODEL // LANES
N_HEADS = 8
HEAD_DIM = 128
LRU_BLOCKS = 8
LRU_BLOCK_W = D_MODEL // LRU_BLOCKS
CONV_W = 4
LRU_C = 8.0
SB_SCALE = HEAD_DIM ** -0.5
PAGE_SIZE = 128
N_KEYS = 128
N_EXPERTS = N_KEYS * N_KEYS
HALF_EXPERTS = N_EXPERTS // 2
PEER_HEADS = 8
PEER_TOPK = 16
N_PAIRS = PEER_HEADS * PEER_TOPK
D_QUERY = 256
D_HALF = D_QUERY // 2
EPS = 1e-6
NEG_INF = float("-inf")

ROW_TILE = 256
LRU_TILE = 256
ATT_TILE = 256
TOPK_TILE = 256
PEER_TILE = 32
VMEM_LIMIT = 56 * 1024 * 1024

_NT = (((1,), (1,)), ((), ()))


def _params(*sem):
    return pltpu.CompilerParams(dimension_semantics=sem, vmem_limit_bytes=VMEM_LIMIT)


def _resident(shape):
    n = len(shape)
    return pl.BlockSpec(shape, lambda *_: (0,) * n, pipeline_mode=pl.Buffered(1))


def _mod_spec(per_seq, tile, rows_per_seq):
    if per_seq:
        tiles_per_seq = rows_per_seq // tile
        return pl.BlockSpec((None, 1, D_MODEL), lambda i: (i // tiles_per_seq, 0, 0))
    return pl.BlockSpec((tile, D_MODEL), lambda i: (i, 0))


def _ada_body(c_ref, w_ref, b_ref, o_ref):
    c = c_ref[...]
    s = c * jax.nn.sigmoid(c)
    o_ref[...] = jnp.dot(s.astype(BF16), w_ref[...], preferred_element_type=F32) + b_ref[...]


def _ada(c, w_bf, b):
    r = c.shape[0]
    n_out = w_bf.shape[1]
    return pl.pallas_call(
        _ada_body,
        grid=(n_out // D_MODEL,),
        in_specs=[pl.BlockSpec((r, D_MODEL), lambda j: (0, 0)),
                  pl.BlockSpec((D_MODEL, D_MODEL), lambda j: (0, j)),
                  pl.BlockSpec((1, D_MODEL), lambda j: (0, j))],
        out_specs=pl.BlockSpec((r, D_MODEL), lambda j: (0, j)),
        out_shape=jax.ShapeDtypeStruct((r, n_out), F32),
        compiler_params=_params("arbitrary"),
        name="ada_mod",
    )(c, w_bf, b)


def _modulated_norm(x, g, scale, shift):
    ms = jnp.mean(x * x, axis=-1, keepdims=True)
    return (x * lax.rsqrt(ms + EPS) * g) * (1.0 + scale) + shift


def _inproj_body(x_ref, sh_ref, sc_ref, g_ref, w_ref, *out_refs):
    h = _modulated_norm(x_ref[...], g_ref[...], sc_ref[...], sh_ref[...]).astype(BF16)
    for i, o_ref in enumerate(out_refs):
        o_ref[...] = jnp.dot(h, w_ref[:, i * D_MODEL:(i + 1) * D_MODEL], preferred_element_type=F32)


def _inproj(x, shift, scale, g, w_bf, per_seq, rows_per_seq):
    n = x.shape[0]
    tile = min(ROW_TILE, n)
    n_out = w_bf.shape[1] // D_MODEL
    row = pl.BlockSpec((tile, D_MODEL), lambda i: (i, 0))
    mod = _mod_spec(per_seq, tile, rows_per_seq)
    return pl.pallas_call(
        _inproj_body,
        grid=(n // tile,),
        in_specs=[row, mod, mod, _resident((1, D_MODEL)), _resident(w_bf.shape)],
        out_specs=[row] * n_out,
        out_shape=[jax.ShapeDtypeStruct((n, D_MODEL), F32)] * n_out,
        compiler_params=_params("arbitrary"),
        name="in_proj",
    )(x, shift, scale, g, w_bf)


def _lru_body(xl_ref, cprev_ref, h0_ref, cw_ref, cb_ref, wa_ref, wx_ref, ba_ref, bx_ref, lam_ref,
              yl_ref, cnew_ref, hl_ref, xbuf, a_s, b_s, hcar, *, tt, t_last):
    j = pl.program_id(1)

    @pl.when(j == 0)
    def _():
        xbuf[0:SUBLANES, :] = cprev_ref[...]
        hcar[...] = jnp.broadcast_to(h0_ref[...], (SUBLANES, D_MODEL))

    xbuf[SUBLANES:SUBLANES + tt, :] = xl_ref[...]
    xc = cb_ref[...]
    for i in range(CONV_W):
        off = SUBLANES - (CONV_W - 1) + i
        xc = xc + cw_ref[i:i + 1, :] * xbuf[off:off + tt, :]
    xcb = xc.astype(BF16)
    r_parts, i_parts = [], []
    for g in range(LRU_BLOCKS):
        xg = xcb[:, g * LRU_BLOCK_W:(g + 1) * LRU_BLOCK_W]
        r_parts.append(jnp.dot(xg, wa_ref[g], preferred_element_type=F32))
        i_parts.append(jnp.dot(xg, wx_ref[g], preferred_element_type=F32))
    r = jax.nn.sigmoid(jnp.concatenate(r_parts, axis=1) + ba_ref[...])
    ig = jax.nn.sigmoid(jnp.concatenate(i_parts, axis=1) + bx_ref[...])
    nl = -lam_ref[...]
    softplus = jnp.maximum(nl, 0.0) + jnp.log1p(jnp.exp(-jnp.abs(nl)))
    log_a = (-LRU_C) * r * softplus
    a = jnp.exp(log_a)
    a_s[...] = a
    b_s[...] = jnp.sqrt(-jnp.tanh(log_a) * (1.0 + a * a)) * ig * xc

    sub = lax.broadcasted_iota(I32, (SUBLANES, D_MODEL), 0)

    def slab(i, hc):
        r0 = pl.multiple_of(i * SUBLANES, SUBLANES)
        a = a_s[pl.ds(r0, SUBLANES), :]
        b = b_s[pl.ds(r0, SUBLANES), :]
        for s in (1, 2, 4):
            m = sub >= s
            b = jnp.where(m, b + a * pltpu.roll(b, s, 0), b)
            a = jnp.where(m, a * pltpu.roll(a, s, 0), a)
        h = a * hc + b
        yl_ref[pl.ds(r0, SUBLANES), :] = h
        return jnp.broadcast_to(h[SUBLANES - 1:SUBLANES, :], (SUBLANES, D_MODEL))

    hcar[...] = lax.fori_loop(0, tt // SUBLANES, slab, hcar[...])

    @pl.when(j == pl.num_programs(1) - 1)
    def _():
        cnew_ref[...] = xbuf[t_last:t_last + SUBLANES, :]
        hl_ref[...] = yl_ref[t_last - 1:t_last, :]

    xbuf[0:SUBLANES, :] = xbuf[tt:tt + SUBLANES, :]


def _lru(xl, cprev8, h0, cw, cb, wa_bf, wx_bf, ba, bx, lam, t_real):
    nb, t_pad, _ = xl.shape
    tt = min(LRU_TILE, t_pad)
    nt = t_pad // tt
    t_last = t_real - (nt - 1) * tt
    seq = lambda rows: pl.BlockSpec((None, rows, D_MODEL), lambda b, j: (b, 0, 0))
    body = functools.partial(_lru_body, tt=tt, t_last=t_last)
    return pl.pallas_call(
        body,
        grid=(nb, nt),
        in_specs=[pl.BlockSpec((None, tt, D_MODEL), lambda b, j: (b, j, 0)), seq(SUBLANES), seq(1),
                  _resident(cw.shape), _resident(cb.shape), _resident(wa_bf.shape), _resident(wx_bf.shape),
                  _resident(ba.shape), _resident(bx.shape), _resident(lam.shape)],
        out_specs=[pl.BlockSpec((None, tt, D_MODEL), lambda b, j: (b, j, 0)), seq(SUBLANES), seq(1)],
        out_shape=[jax.ShapeDtypeStruct((nb, t_pad, D_MODEL), F32),
                   jax.ShapeDtypeStruct((nb, SUBLANES, D_MODEL), F32),
                   jax.ShapeDtypeStruct((nb, 1, D_MODEL), F32)],
        scratch_shapes=[pltpu.VMEM((tt + SUBLANES, D_MODEL), F32), pltpu.VMEM((tt, D_MODEL), F32),
                        pltpu.VMEM((tt, D_MODEL), F32), pltpu.VMEM((SUBLANES, D_MODEL), F32)],
        compiler_params=_params("arbitrary", "arbitrary"),
        name="rglru",
    )(xl, cprev8, h0, cw, cb, wa_bf, wx_bf, ba, bx, lam)


def _suffix_matrix():
    kp = np.arange(2 * LANES)[:, None] % LANES
    kk = np.arange(2 * LANES)[None, :]
    m = np.where(kk < LANES, kp > kk, True)
    return jnp.asarray(m, dtype=BF16)


def _sb_block(z, valid, acc, vh, u):
    nz = -z
    soft = jnp.log1p(jnp.exp(jnp.minimum(z, nz)))
    log_beta = jnp.minimum(z, 0.0) - soft
    log_keep = jnp.minimum(nz, 0.0) - soft
    if valid is not None:
        log_keep = jnp.where(valid, log_keep, 0.0)
    hi = log_keep.astype(BF16)
    lo = (log_keep - hi.astype(F32)).astype(BF16)
    res = jnp.dot(jnp.concatenate([hi, lo], axis=1), u, preferred_element_type=F32)
    later = res[:, :LANES] + acc
    w = jnp.exp(log_beta + later)
    if valid is not None:
        w = jnp.where(valid, w, 0.0)
    return jnp.dot(w.astype(BF16), vh, preferred_element_type=F32), res[:, LANES:]


def _attn_body(qi_ref, kj_ref, bias_ref, q_ref, k_ref, v_ref, u_ref, o_ref, acc_s, out_s, *, tile):
    s = pl.program_id(1)
    qi = qi_ref[s]
    kj = kj_ref[s]

    @pl.when(kj == qi)
    def _():
        acc_s[...] = jnp.zeros_like(acc_s)
        out_s[...] = jnp.zeros_like(out_s)

    u = u_ref[...]
    col_minus_row = (lax.broadcasted_iota(I32, (tile, LANES), 1)
                     - lax.broadcasted_iota(I32, (tile, LANES), 0))
    base = (qi - kj) * tile
    for h in range(N_HEADS):
        hs = slice(h * HEAD_DIM, (h + 1) * HEAD_DIM)
        qh = q_ref[:, hs].astype(BF16)
        acc = acc_s[h]
        out = out_s[:, hs]
        for sb in reversed(range(tile // LANES)):
            rs = slice(sb * LANES, (sb + 1) * LANES)
            kh = k_ref[rs, hs].astype(BF16)
            vh = v_ref[rs, hs].astype(BF16)
            z = lax.dot_general(qh, kh, _NT, preferred_element_type=F32) * SB_SCALE + bias_ref[h]
            valid = col_minus_row < base - sb * LANES
            d_out, d_acc = _sb_block(z, valid, acc, vh, u)
            out = out + d_out
            acc = acc + d_acc
        acc_s[h] = acc
        out_s[:, hs] = out

    @pl.when(kj == 0)
    def _():
        o_ref[...] = out_s[...]


def _attn_prompt(q, k, v, bias):
    nb, s_len, _ = q.shape
    tile = min(ATT_TILE, s_len)
    nq = s_len // tile
    qi = np.concatenate([np.full(i + 1, i) for i in range(nq)]).astype(np.int32)
    kj = np.concatenate([np.arange(i, -1, -1) for i in range(nq)]).astype(np.int32)
    qspec = pl.BlockSpec((None, tile, D_MODEL), lambda b, s, qi_r, kj_r, bias_r: (b, qi_r[s], 0))
    kspec = pl.BlockSpec((None, tile, D_MODEL), lambda b, s, qi_r, kj_r, bias_r: (b, kj_r[s], 0))
    uspec = pl.BlockSpec((2 * LANES, 2 * LANES), lambda b, s, *_: (0, 0))
    return pl.pallas_call(
        functools.partial(_attn_body, tile=tile),
        grid_spec=pltpu.PrefetchScalarGridSpec(
            num_scalar_prefetch=3,
            grid=(nb, len(qi)),
            in_specs=[qspec, kspec, kspec, uspec],
            out_specs=qspec,
            scratch_shapes=[pltpu.VMEM((N_HEADS, tile, LANES), F32), pltpu.VMEM((tile, D_MODEL), F32)],
        ),
        out_shape=jax.ShapeDtypeStruct((nb, s_len, D_MODEL), F32),
        compiler_params=_params("arbitrary", "arbitrary"),
        name="sb_attn_prompt",
    )(jnp.asarray(qi), jnp.asarray(kj), bias, q, k, v, _suffix_matrix())


def _sattn_body(pt_ref, bias_ref, q_ref, kn_ref, vn_ref, kp_ref, vp_ref, u_ref, o_ref,
                acc_s, out_s, kpad, vpad, *, t_new):
    n = pl.program_id(0)
    p = pl.program_id(1)
    u = u_ref[...]
    q_rows = q_ref.shape[0]

    def sweep(k_src, v_src, valid):
        for h in range(N_HEADS):
            hs = slice(h * HEAD_DIM, (h + 1) * HEAD_DIM)
            qh = q_ref[:, hs].astype(BF16)
            kh = k_src[pl.ds(h, PAGE_SIZE, stride=N_HEADS), :].astype(BF16)
            vh = v_src[pl.ds(h, PAGE_SIZE, stride=N_HEADS), :].astype(BF16)
            z = lax.dot_general(qh, kh, _NT, preferred_element_type=F32) * SB_SCALE + bias_ref[h]
            d_out, d_acc = _sb_block(z, valid, acc_s[h], vh, u)
            out_s[:, hs] = out_s[:, hs] + d_out
            acc_s[h] = acc_s[h] + d_acc

    @pl.when((n == 0) & (p == 0))
    def _():
        kpad[...] = jnp.zeros_like(kpad)
        vpad[...] = jnp.zeros_like(vpad)

    @pl.when(p == 0)
    def _():
        kpad[0:t_new * N_HEADS, :] = kn_ref[...]
        vpad[0:t_new * N_HEADS, :] = vn_ref[...]
        acc_s[...] = jnp.zeros_like(acc_s)
        out_s[...] = jnp.zeros_like(out_s)
        valid = (lax.broadcasted_iota(I32, (q_rows, LANES), 1)
                 < lax.broadcasted_iota(I32, (q_rows, LANES), 0))
        sweep(kpad, vpad, valid)

    @pl.when(p > 0)
    def _():
        sweep(kp_ref, vp_ref, None)

    @pl.when(p == pl.num_programs(1) - 1)
    def _():
        o_ref[...] = out_s[...]


def _attn_sample(q8, k_new, v_new, bias, cache_k, cache_v, page_table):
    n, q_rows, _ = q8.shape
    n_pages = page_table.shape[1]
    t_new = k_new.shape[1] // N_HEADS
    page_rows = PAGE_SIZE * N_HEADS

    def page_map(i, p, pt, b):
        logical = n_pages - jnp.maximum(p, 1)
        return (pt[i, logical], 0, 0)

    seq = lambda rows, cols: pl.BlockSpec((None, rows, cols), lambda i, p, pt, b: (i, 0, 0))
    page = pl.BlockSpec((None, page_rows, HEAD_DIM), page_map)
    return pl.pallas_call(
        functools.partial(_sattn_body, t_new=t_new),
        grid_spec=pltpu.PrefetchScalarGridSpec(
            num_scalar_prefetch=2,
            grid=(n, n_pages + 1),
            in_specs=[seq(q_rows, D_MODEL), seq(t_new * N_HEADS, HEAD_DIM), seq(t_new * N_HEADS, HEAD_DIM),
                      page, page, pl.BlockSpec((2 * LANES, 2 * LANES), lambda i, p, pt, b: (0, 0))],
            out_specs=seq(q_rows, D_MODEL),
            scratch_shapes=[pltpu.VMEM((N_HEADS, q_rows, LANES), F32), pltpu.VMEM((q_rows, D_MODEL), F32),
                            pltpu.VMEM((page_rows, HEAD_DIM), F32), pltpu.VMEM((page_rows, HEAD_DIM), F32)],
        ),
        out_shape=jax.ShapeDtypeStruct((n, q_rows, D_MODEL), F32),
        compiler_params=_params("arbitrary", "arbitrary"),
        name="sb_attn_sample",
    )(page_table, bias, q8, k_new, v_new, cache_k, cache_v, _suffix_matrix())


def _outproj_body(x_ref, yl_ref, ya_ref, gl_ref, ga_ref, g1_ref, sh2_ref, sc2_ref, gf_ref,
                  wl_ref, wa_ref, wo_ref, wq_ref, x1_ref, h2_ref, qp_ref):
    yl = jnp.dot(yl_ref[...].astype(BF16), wl_ref[...], preferred_element_type=F32)
    ya = jnp.dot(ya_ref[...].astype(BF16), wa_ref[...], preferred_element_type=F32)
    merged = jax.nn.sigmoid(gl_ref[...]) * yl + jax.nn.sigmoid(ga_ref[...]) * ya
    x1 = x_ref[...] + g1_ref[...] * jnp.dot(merged.astype(BF16), wo_ref[...], preferred_element_type=F32)
    x1_ref[...] = x1
    h2 = _modulated_norm(x1, gf_ref[...], sc2_ref[...], sh2_ref[...])
    h2_ref[...] = h2
    qp_ref[...] = jnp.dot(h2.astype(BF16), wq_ref[...], preferred_element_type=F32)


def _outproj(x, yl, ya, gl, ga, gate1, shift2, scale2, g_ffn, wl_bf, wa_bf, wo_bf, wq_bf, per_seq, rows_per_seq):
    n = x.shape[0]
    tile = min(ROW_TILE, n)
    dq = wq_bf.shape[1]
    row = pl.BlockSpec((tile, D_MODEL), lambda i: (i, 0))
    mod = _mod_spec(per_seq, tile, rows_per_seq)
    return pl.pallas_call(
        _outproj_body,
        grid=(n // tile,),
        in_specs=[row] * 5 + [mod] * 3 + [_resident((1, D_MODEL)), _resident(wl_bf.shape), _resident(wa_bf.shape),
                                          _resident(wo_bf.shape), _resident(wq_bf.shape)],
        out_specs=[row, row, pl.BlockSpec((tile, dq), lambda i: (i, 0))],
        out_shape=[jax.ShapeDtypeStruct((n, D_MODEL), F32), jax.ShapeDtypeStruct((n, D_MODEL), F32),
                   jax.ShapeDtypeStruct((n, dq), F32)],
        compiler_params=_params("arbitrary"),
        name="out_proj",
    )(x, yl, ya, gl, ga, gate1, shift2, scale2, g_ffn, wl_bf, wa_bf, wo_bf, wq_bf)


def _colmax(x):
    return jnp.max(x, axis=0, keepdims=True)


def _colmin(x):
    return jnp.min(x, axis=0, keepdims=True)


def _top16_keys(s):
    t = s.shape[1]
    key = lax.broadcasted_iota(I32, s.shape, 0)
    slot = lax.broadcasted_iota(I32, (PEER_TOPK, t), 0)
    vals = jnp.zeros((PEER_TOPK, t), F32)
    idxs = jnp.zeros((PEER_TOPK, t), I32)
    for r in range(PEER_TOPK):
        m = _colmax(s)
        idx = _colmin(jnp.where(s == m, key, N_KEYS))
        vals = jnp.where(slot == r, m, vals)
        idxs = jnp.where(slot == r, idx, idxs)
        s = jnp.where(key == idx, NEG_INF, s)
    return vals, idxs


def _candidates(v1, i1, v2, i2):
    t = v1.shape[1]
    r8 = lax.broadcasted_iota(I32, (SUBLANES, t), 0)
    r16 = lax.broadcasted_iota(I32, (2 * SUBLANES, t), 0)
    lo, hi = slice(0, SUBLANES), slice(SUBLANES, 2 * SUBLANES)
    row = lambda x, a: x[a:a + 1, :]
    vals, eids, flats = [], [], []

    def add(v, e, f, keep=None):
        vals.append(v if keep is None else jnp.where(keep, v, NEG_INF))
        eids.append(e)
        flats.append(f)

    add(row(v1, 0) + v2, row(i1, 0) * N_KEYS + i2, r16)
    add(row(v1, 1) + v2[lo], row(i1, 1) * N_KEYS + i2[lo], PEER_TOPK + r8)
    for a in (2, 3):
        add(row(v1, a) + v2[lo], row(i1, a) * N_KEYS + i2[lo], a * PEER_TOPK + r8)
    add(v1[hi] + row(v2, 0), i1[hi] * N_KEYS + row(i2, 0), (r8 + SUBLANES) * PEER_TOPK)
    for b in (0, 1, 2):
        add(v1[lo] + row(v2, b), i1[lo] * N_KEYS + row(i2, b), r8 * PEER_TOPK + b, keep=r8 >= 4)
    return (jnp.concatenate(vals, axis=0), jnp.concatenate(eids, axis=0), jnp.concatenate(flats, axis=0))


def _top16_candidates(cand, eid, flat):
    t = cand.shape[1]
    slot = lax.broadcasted_iota(I32, (PEER_TOPK, t), 0)
    best = jnp.zeros((PEER_TOPK, t), F32)
    sel = jnp.zeros((PEER_TOPK, t), I32)
    for r in range(PEER_TOPK):
        m = _colmax(cand)
        f = _colmin(jnp.where(cand == m, flat, PEER_TOPK * PEER_TOPK))
        hit = flat == f
        e = _colmax(jnp.where(hit, eid, -1))
        best = jnp.where(slot == r, m, best)
        sel = jnp.where(slot == r, e, sel)
        cand = jnp.where(hit, NEG_INF, cand)
    return best, sel


def _topk_body(q_ref, k1_ref, k2_ref, addr_ref, shift_ref, gate_ref, e_t, g_t):
    def head(h, carry):
        c0 = pl.multiple_of(h * D_QUERY, D_QUERY)
        q1 = q_ref[:, pl.ds(c0, D_HALF)].astype(BF16)
        q2 = q_ref[:, pl.ds(pl.multiple_of(c0 + D_HALF, D_HALF), D_HALF)].astype(BF16)
        s1 = lax.dot_general(k1_ref[h], q1, _NT, preferred_element_type=F32)
        s2 = lax.dot_general(k2_ref[h], q2, _NT, preferred_element_type=F32)
        v1, i1 = _top16_keys(s1)
        v2, i2 = _top16_keys(s2)
        best, sel = _top16_candidates(*_candidates(v1, i1, v2, i2))
        ex = jnp.exp(best - best[0:1, :])
        r0 = pl.multiple_of(h * PEER_TOPK, PEER_TOPK)
        e_t[pl.ds(r0, PEER_TOPK), :] = sel
        g_t[pl.ds(r0, PEER_TOPK), :] = ex / jnp.sum(ex, axis=0, keepdims=True)
        return carry

    lax.fori_loop(0, PEER_HEADS, head, 0)
    e = e_t[...].T
    addr_ref[...] = (e & (HALF_EXPERTS - 1)) * N_CHUNK
    shift_ref[...] = (e >> 13) << 4
    gate_ref[...] = g_t[...].T


def _topk(qp, k1_bf, k2_bf):
    n = qp.shape[0]
    tile = min(TOPK_TILE, n)
    out = pl.BlockSpec((tile, N_PAIRS), lambda i: (i, 0))
    return pl.pallas_call(
        _topk_body,
        grid=(n // tile,),
        in_specs=[pl.BlockSpec((tile, qp.shape[1]), lambda i: (i, 0)), _resident(k1_bf.shape), _resident(k2_bf.shape)],
        out_specs=[out, out, out],
        out_shape=[jax.ShapeDtypeStruct((n, N_PAIRS), I32), jax.ShapeDtypeStruct((n, N_PAIRS), I32),
                   jax.ShapeDtypeStruct((n, N_PAIRS), F32)],
        scratch_shapes=[pltpu.VMEM((N_PAIRS, tile), I32), pltpu.VMEM((N_PAIRS, tile), F32)],
        compiler_params=_params("arbitrary"),
        name="peer_topk",
    )(qp, k1_bf, k2_bf)


def _pack_table(t):
    b = lax.bitcast_convert_type(t.astype(BF16), jnp.uint16).astype(jnp.uint32)
    w = (b[:HALF_EXPERTS] << 16) | b[HALF_EXPERTS:]
    return w.reshape(HALF_EXPERTS * N_CHUNK, LANES)


def _expert_row(tab_ref, addr, shift):
    w = tab_ref[pl.ds(pl.multiple_of(addr, N_CHUNK), N_CHUNK), :]
    return lax.bitcast_convert_type((w << shift.astype(jnp.uint32)) & jnp.uint32(0xFFFF0000), F32)


def _butterfly(parts, index, steps, axis):
    for step in steps:
        m = (index & step) == 0
        parts = [jnp.where(m, a, b) + pltpu.roll(jnp.where(m, b, a), step, axis)
                 for a, b in zip(parts[0::2], parts[1::2])]
    return parts[0]


def _peer_u_body(addr_ref, shift_ref, h_ref, gate_ref, tab_ref, coef_ref, *, tile):
    sub = lax.broadcasted_iota(I32, (SUBLANES, LANES), 0)
    lane = lax.broadcasted_iota(I32, (SUBLANES, LANES), 1)
    diag = (lane & (SUBLANES - 1)) == sub

    def token(t, carry):
        h = h_ref[pl.ds(pl.multiple_of(t * N_CHUNK, N_CHUNK), N_CHUNK), :]
        groups = []
        for g in range(N_PAIRS // SUBLANES):
            prods = []
            for r in range(SUBLANES):
                k = t * N_PAIRS + g * SUBLANES + r
                prods.append(_expert_row(tab_ref, addr_ref[0, 0, k], shift_ref[0, 0, k]) * h)
            groups.append(_butterfly(prods, sub, (1, 2, 4), 0))
        v = _butterfly(groups, lane, (8, 16, 32, 64), 1)
        for s in (1, 2, 4):
            v = v + pltpu.roll(v, s, 1)
        v = pltpu.roll(v, LANES - SUBLANES + 1, 1, stride=1, stride_axis=0)
        score = jnp.sum(jnp.where(diag, v, 0.0), axis=0, keepdims=True)
        coef_ref[pl.ds(t, 1), :] = score
        return carry

    lax.fori_loop(0, tile, token, 0)
    coef_ref[...] = gate_ref[...] * jax.nn.gelu(coef_ref[...])


def _peer_v_body(addr_ref, shift_ref, coef_ref, tab_ref, o_ref, *, tile):
    def token(t, carry):
        accs = [jnp.zeros((N_CHUNK, LANES), F32) for _ in range(4)]
        for j in range(N_PAIRS):
            k = t * N_PAIRS + j
            accs[j % 4] = accs[j % 4] + coef_ref[0, 0, k] * _expert_row(tab_ref, addr_ref[0, 0, k], shift_ref[0, 0, k])
        o_ref[pl.ds(pl.multiple_of(t * N_CHUNK, N_CHUNK), N_CHUNK), :] = (accs[0] + accs[1]) + (accs[2] + accs[3])
        return carry

    lax.fori_loop(0, tile, token, 0)


def _peer_apply(h2, addr, shift, gate, tab_u, tab_v):
    n = h2.shape[0]
    tile = min(PEER_TILE, n)
    steps = n // tile
    smem = pl.BlockSpec((1, 1, tile * N_PAIRS), lambda i: (i, 0, 0), memory_space=pltpu.SMEM)
    rows = pl.BlockSpec((tile * N_CHUNK, LANES), lambda i: (i, 0))
    pairs = pl.BlockSpec((tile, N_PAIRS), lambda i: (i, 0))
    addr_s = addr.reshape(steps, 1, tile * N_PAIRS)
    shift_s = shift.reshape(steps, 1, tile * N_PAIRS)
    coef = pl.pallas_call(
        functools.partial(_peer_u_body, tile=tile),
        grid=(steps,),
        in_specs=[smem, smem, rows, pairs, _resident(tab_u.shape)],
        out_specs=pairs,
        out_shape=jax.ShapeDtypeStruct((n, N_PAIRS), F32),
        compiler_params=_params("arbitrary"),
        name="peer_score",
    )(addr_s, shift_s, h2.reshape(n * N_CHUNK, LANES), gate, tab_u)
    out = pl.pallas_call(
        functools.partial(_peer_v_body, tile=tile),
        grid=(steps,),
        in_specs=[smem, smem, smem, _resident(tab_v.shape)],
        out_specs=rows,
        out_shape=jax.ShapeDtypeStruct((n * N_CHUNK, LANES), F32),
        compiler_params=_params("arbitrary"),
        name="peer_mix",
    )(addr_s, shift_s, coef.reshape(steps, 1, tile * N_PAIRS), tab_v)
    return out.reshape(n, D_MODEL)


def _final_body(x_ref, p_ref, g2_ref, gf_ref, y_ref):
    x = x_ref[...] + g2_ref[...] * p_ref[...]
    ms = jnp.mean(x * x, axis=-1, keepdims=True)
    y_ref[...] = x * lax.rsqrt(ms + EPS) * gf_ref[...]


def _final(x1, peer_out, gate2, g_final, per_seq, rows_per_seq):
    n = x1.shape[0]
    tile = min(ROW_TILE, n)
    row = pl.BlockSpec((tile, D_MODEL), lambda i: (i, 0))
    return pl.pallas_call(
        _final_body,
        grid=(n // tile,),
        in_specs=[row, row, _mod_spec(per_seq, tile, rows_per_seq), _resident((1, D_MODEL))],
        out_specs=row,
        out_shape=jax.ShapeDtypeStruct((n, D_MODEL), F32),
        compiler_params=_params("arbitrary"),
        name="final_norm",
    )(x1, peer_out, gate2, g_final)


def _group_layer(x, mod, per_seq, attend, conv_prev, h0, p):
    nseq, t, _ = x.shape
    n = nseq * t
    xf = x.reshape(n, D_MODEL)
    if per_seq:
        mods = [m.reshape(nseq, 1, D_MODEL) for m in jnp.split(mod, 6, axis=-1)]
    else:
        mods = [jnp.repeat(m, t, axis=0) for m in jnp.split(mod, 6, axis=-1)]
    shift1, scale1, gate1, shift2, scale2, gate2 = mods

    xl, q, k, v, gl, ga = _inproj(xf, shift1, scale1, p["g_mix"], p["w_in"], per_seq, t)

    t_pad = -(-t // SUBLANES) * SUBLANES
    xl3 = jnp.pad(xl.reshape(nseq, t, D_MODEL), ((0, 0), (0, t_pad - t), (0, 0)))
    cprev8 = jnp.pad(conv_prev, ((0, 0), (SUBLANES - (CONV_W - 1), 0), (0, 0)))
    yl, conv8, h_last = _lru(xl3, cprev8, h0.reshape(nseq, 1, D_MODEL), p["conv_w"], p["conv_b"],
                             p["lru_wa"], p["lru_wx"], p["lru_ba"], p["lru_bx"], p["lru_lambda"], t)
    yl = yl[:, :t].reshape(n, D_MODEL)
    conv_new = conv8[:, SUBLANES - (CONV_W - 1):]
    h_new = h_last.reshape(nseq, D_MODEL)

    ya = attend(q, k, v).reshape(n, D_MODEL)

    x1, h2, qp = _outproj(xf, yl, ya, gl, ga, gate1, shift2, scale2, p["g_ffn"], p["w_lru_out"], p["w_att_out"],
                          p["w_o"], p["peer_wq"], per_seq, t)
    addr, shift, gate = _topk(qp, p["peer_k1"], p["peer_k2"])
    peer_out = _peer_apply(h2, addr, shift, gate, p["peer_u"], p["peer_v"])
    y = _final(x1, peer_out, gate2, p["g_final"], per_seq, t)
    kv_shape = (nseq, t, N_HEADS, HEAD_DIM)
    return y.reshape(nseq, t, D_MODEL), k.reshape(kv_shape), v.reshape(kv_shape), conv_new, h_new


def kernel(x_prompt, x_sample, cache_k, cache_v, state_conv, state_lru, page_table, c_prompt, c_sample,
           g_mix, w_ada, b_ada, w_in, conv_w, conv_b, lru_wa, lru_ba, lru_wx, lru_bx, lru_lambda, sb_bias,
           w_lru_out, w_att_out, w_o, g_ffn, peer_wq, peer_k1, peer_k2, peer_u, peer_v, g_final):
    depth = w_in.shape[0]
    nb = x_prompt.shape[0]
    nd, t_dec, _ = x_sample.shape
    n_phys = cache_k.shape[1]
    row = lambda a: a.reshape(1, D_MODEL)
    xp, xs = x_prompt, x_sample
    outs = [[] for _ in range(8)]
    for l in range(depth):
        p = {"g_mix": row(g_mix[l]), "w_in": w_in[l].astype(BF16), "conv_w": conv_w[l], "conv_b": row(conv_b[l]),
             "lru_wa": lru_wa[l].astype(BF16), "lru_wx": lru_wx[l].astype(BF16), "lru_ba": row(lru_ba[l]),
             "lru_bx": row(lru_bx[l]), "lru_lambda": row(lru_lambda[l]), "w_lru_out": w_lru_out[l].astype(BF16),
             "w_att_out": w_att_out[l].astype(BF16), "w_o": w_o[l].astype(BF16), "g_ffn": row(g_ffn[l]),
             "peer_wq": peer_wq[l].astype(BF16), "peer_k1": peer_k1[l].astype(BF16),
             "peer_k2": peer_k2[l].astype(BF16), "peer_u": _pack_table(peer_u[l]), "peer_v": _pack_table(peer_v[l]),
             "g_final": row(g_final)}
        bias = sb_bias[l]
        mod = _ada(jnp.concatenate([c_prompt, c_sample], axis=0), w_ada[l].astype(BF16), b_ada[l].reshape(1, -1))

        def attend_prompt(q, k, v):
            shape = (nb, q.shape[0] // nb, D_MODEL)
            return _attn_prompt(q.reshape(shape), k.reshape(shape), v.reshape(shape), bias)

        def attend_sample(q, k, v, l=l):
            q8 = jnp.pad(q.reshape(nd, t_dec, D_MODEL), ((0, 0), (0, SUBLANES - t_dec), (0, 0)))
            paged = lambda c: c[l].reshape(n_phys, PAGE_SIZE * N_HEADS, HEAD_DIM)
            new = lambda a: a.reshape(nd, t_dec * N_HEADS, HEAD_DIM)
            out = _attn_sample(q8, new(k), new(v), bias, paged(cache_k), paged(cache_v), page_table)
            return out[:, :t_dec]

        zeros_conv = jnp.zeros((nb, CONV_W - 1, D_MODEL), F32)
        zeros_h = jnp.zeros((nb, D_MODEL), F32)
        xp, kp, vp, cp, hp = _group_layer(xp, mod[:nb], True, attend_prompt, zeros_conv, zeros_h, p)
        xs, ks, vs, cs, hs = _group_layer(xs, mod[nb:], False, attend_sample, state_conv[l], state_lru[l], p)
        for lst, val in zip(outs, (kp, vp, cp, hp, ks, vs, cs, hs)):
            lst.append(val)
    assert depth == 1
    return (xp, xs) + tuple(jnp.stack(o) for o in outs)
```

```python
import functools

import numpy as np
import jax
import jax.numpy as jnp
from jax import lax
from jax.experimental import pallas as pl
from jax.experimental.pallas import tpu as pltpu

F32 = jnp.float32
BF16 = jnp.bfloat16
I32 = jnp.int32

LANES = 128
SUBLANES = 8
D_MODEL = 1024
N_CHUNK = D_MODEL // LANES
N_HEADS = 8
HEAD_DIM = 128
LRU_BLOCKS = 8
LRU_BLOCK_W = D_MODEL // LRU_BLOCKS
CONV_W = 4
LRU_C = 8.0
SB_SCALE = HEAD_DIM ** -0.5
LOG2E = 1.4426950408889634
PAGE_SIZE = 128
N_KEYS = 128
N_EXPERTS = N_KEYS * N_KEYS
HALF_EXPERTS = N_EXPERTS // 2
PEER_HEADS = 8
PEER_TOPK = 16
N_PAIRS = PEER_HEADS * PEER_TOPK
D_QUERY = 256
D_HALF = D_QUERY // 2
EPS = 1e-6
NEG_INF = float("-inf")

ROW_TILE = 256
LRU_TILE = 256
ATT_TILE = 256
TOPK_TILE = 256
PEER_TILE = 32
VMEM_LIMIT = 56 * 1024 * 1024

_NT = (((1,), (1,)), ((), ()))


def _params(*sem):
    return pltpu.CompilerParams(dimension_semantics=sem, vmem_limit_bytes=VMEM_LIMIT)


def _resident(shape):
    n = len(shape)
    return pl.BlockSpec(shape, lambda *_: (0,) * n, pipeline_mode=pl.Buffered(1))


def _mod_spec(per_seq, tile, rows_per_seq):
    if per_seq:
        tiles_per_seq = rows_per_seq // tile
        return pl.BlockSpec((None, 1, D_MODEL), lambda i: (i // tiles_per_seq, 0, 0))
    return pl.BlockSpec((tile, D_MODEL), lambda i: (i, 0))


def _ada_body(c_ref, w_ref, b_ref, o_ref):
    c = c_ref[...]
    s = c * jax.nn.sigmoid(c)
    o_ref[...] = jnp.dot(s.astype(BF16), w_ref[...], preferred_element_type=F32) + b_ref[...]


def _ada(c, w_bf, b):
    r = c.shape[0]
    n_out = w_bf.shape[1]
    return pl.pallas_call(
        _ada_body,
        grid=(n_out // D_MODEL,),
        in_specs=[pl.BlockSpec((r, D_MODEL), lambda j: (0, 0)),
                  pl.BlockSpec((D_MODEL, D_MODEL), lambda j: (0, j)),
                  pl.BlockSpec((1, D_MODEL), lambda j: (0, j))],
        out_specs=pl.BlockSpec((r, D_MODEL), lambda j: (0, j)),
        out_shape=jax.ShapeDtypeStruct((r, n_out), F32),
        compiler_params=_params("arbitrary"),
        name="ada_mod",
    )(c, w_bf, b)


def _modulated_norm(x, g, scale, shift):
    ms = jnp.mean(x * x, axis=-1, keepdims=True)
    return (x * lax.rsqrt(ms + EPS) * g) * (1.0 + scale) + shift


def _inproj_body(x_ref, sh_ref, sc_ref, g_ref, w_ref, *out_refs):
    h = _modulated_norm(x_ref[...], g_ref[...], sc_ref[...], sh_ref[...]).astype(BF16)
    for i, o_ref in enumerate(out_refs):
        o_ref[...] = jnp.dot(h, w_ref[:, i * D_MODEL:(i + 1) * D_MODEL], preferred_element_type=F32)


def _inproj(x, shift, scale, g, w_bf, per_seq, rows_per_seq):
    n = x.shape[0]
    tile = min(ROW_TILE, n)
    n_out = w_bf.shape[1] // D_MODEL
    row = pl.BlockSpec((tile, D_MODEL), lambda i: (i, 0))
    mod = _mod_spec(per_seq, tile, rows_per_seq)
    return pl.pallas_call(
        _inproj_body,
        grid=(n // tile,),
        in_specs=[row, mod, mod, _resident((1, D_MODEL)), _resident(w_bf.shape)],
        out_specs=[row] * n_out,
        out_shape=[jax.ShapeDtypeStruct((n, D_MODEL), F32)] * n_out,
        compiler_params=_params("arbitrary"),
        name="in_proj",
    )(x, shift, scale, g, w_bf)


def _lru_body(xl_ref, cprev_ref, h0_ref, cw_ref, cb_ref, wa_ref, wx_ref, ba_ref, bx_ref, lam_ref,
              yl_ref, cnew_ref, hl_ref, xbuf, a_s, b_s, hcar, *, tt, t_last):
    j = pl.program_id(1)

    @pl.when(j == 0)
    def _():
        xbuf[0:SUBLANES, :] = cprev_ref[...]
        hcar[...] = jnp.broadcast_to(h0_ref[...], (SUBLANES, D_MODEL))

    xbuf[SUBLANES:SUBLANES + tt, :] = xl_ref[...]
    xc = cb_ref[...]
    for i in range(CONV_W):
        off = SUBLANES - (CONV_W - 1) + i
        xc = xc + cw_ref[i:i + 1, :] * xbuf[off:off + tt, :]
    xcb = xc.astype(BF16)
    r_parts, i_parts = [], []
    for g in range(LRU_BLOCKS):
        xg = xcb[:, g * LRU_BLOCK_W:(g + 1) * LRU_BLOCK_W]
        r_parts.append(jnp.dot(xg, wa_ref[g], preferred_element_type=F32))
        i_parts.append(jnp.dot(xg, wx_ref[g], preferred_element_type=F32))
    r = jax.nn.sigmoid(jnp.concatenate(r_parts, axis=1) + ba_ref[...])
    ig = jax.nn.sigmoid(jnp.concatenate(i_parts, axis=1) + bx_ref[...])
    nl = -lam_ref[...]
    softplus = jnp.maximum(nl, 0.0) + jnp.log1p(jnp.exp(-jnp.abs(nl)))
    log_a = (-LRU_C) * r * softplus
    a = jnp.exp(log_a)
    a_s[...] = a
    b_s[...] = jnp.sqrt(-jnp.tanh(log_a) * (1.0 + a * a)) * ig * xc

    sub = lax.broadcasted_iota(I32, (SUBLANES, D_MODEL), 0)

    def slab(i, hc):
        r0 = pl.multiple_of(i * SUBLANES, SUBLANES)
        a = a_s[pl.ds(r0, SUBLANES), :]
        b = b_s[pl.ds(r0, SUBLANES), :]
        for s in (1, 2, 4):
            m = sub >= s
            b = jnp.where(m, b + a * pltpu.roll(b, s, 0), b)
            a = jnp.where(m, a * pltpu.roll(a, s, 0), a)
        h = a * hc + b
        yl_ref[pl.ds(r0, SUBLANES), :] = h
        return jnp.broadcast_to(h[SUBLANES - 1:SUBLANES, :], (SUBLANES, D_MODEL))

    hcar[...] = lax.fori_loop(0, tt // SUBLANES, slab, hcar[...])

    @pl.when(j == pl.num_programs(1) - 1)
    def _():
        cnew_ref[...] = xbuf[t_last:t_last + SUBLANES, :]
        hl_ref[...] = yl_ref[t_last - 1:t_last, :]

    xbuf[0:SUBLANES, :] = xbuf[tt:tt + SUBLANES, :]


def _lru(xl, cprev8, h0, cw, cb, wa_bf, wx_bf, ba, bx, lam, t_real):
    nb, t_pad, _ = xl.shape
    tt = min(LRU_TILE, t_pad)
    nt = t_pad // tt
    t_last = t_real - (nt - 1) * tt
    seq = lambda rows: pl.BlockSpec((None, rows, D_MODEL), lambda b, j: (b, 0, 0))
    body = functools.partial(_lru_body, tt=tt, t_last=t_last)
    return pl.pallas_call(
        body,
        grid=(nb, nt),
        in_specs=[pl.BlockSpec((None, tt, D_MODEL), lambda b, j: (b, j, 0)), seq(SUBLANES), seq(1),
                  _resident(cw.shape), _resident(cb.shape), _resident(wa_bf.shape), _resident(wx_bf.shape),
                  _resident(ba.shape), _resident(bx.shape), _resident(lam.shape)],
        out_specs=[pl.BlockSpec((None, tt, D_MODEL), lambda b, j: (b, j, 0)), seq(SUBLANES), seq(1)],
        out_shape=[jax.ShapeDtypeStruct((nb, t_pad, D_MODEL), F32),
                   jax.ShapeDtypeStruct((nb, SUBLANES, D_MODEL), F32),
                   jax.ShapeDtypeStruct((nb, 1, D_MODEL), F32)],
        scratch_shapes=[pltpu.VMEM((tt + SUBLANES, D_MODEL), F32), pltpu.VMEM((tt, D_MODEL), F32),
                        pltpu.VMEM((tt, D_MODEL), F32), pltpu.VMEM((SUBLANES, D_MODEL), F32)],
        compiler_params=_params("arbitrary", "arbitrary"),
        name="rglru",
    )(xl, cprev8, h0, cw, cb, wa_bf, wx_bf, ba, bx, lam)


def _suffix_matrix():
    kp = np.arange(2 * LANES)[:, None] % LANES
    kk = np.arange(2 * LANES)[None, :]
    m = np.where(kk < LANES, kp > kk, True)
    return jnp.asarray(m, dtype=BF16)


def _sb_logs(z, valid):
    nz = -z
    soft = jnp.log2(1.0 + jnp.exp2(jnp.minimum(z, nz)))
    log_beta = jnp.minimum(z, 0.0) - soft
    log_keep = jnp.minimum(nz, 0.0) - soft
    if valid is not None:
        log_keep = jnp.where(valid, log_keep, 0.0)
    hi = log_keep.astype(BF16)
    lo = (log_keep - hi.astype(F32)).astype(BF16)
    return log_beta, jnp.concatenate([hi, lo], axis=1)


def _sb_sweep(chains, q_of, k_of, v_of, valid_of, bias_ref, u, acc_s, out_s):
    hsl = lambda h: slice(h * HEAD_DIM, (h + 1) * HEAD_DIM)
    heads = sorted({h for h, _ in chains})
    qs = {h: q_of(h) for h in heads}
    zs = [lax.dot_general(qs[h], k_of(h, blk), _NT, preferred_element_type=F32) for h, blk in chains]
    log_betas, splits = [], []
    for (h, blk), z in zip(chains, zs):
        log_beta, hi_lo = _sb_logs(z * (SB_SCALE * LOG2E) + bias_ref[h], valid_of(blk))
        log_betas.append(log_beta)
        splits.append(hi_lo)
    sums = [jnp.dot(x, u, preferred_element_type=F32) for x in splits]
    accs = {h: acc_s[h] for h in heads}
    ws = []
    for (h, blk), log_beta, res in zip(chains, log_betas, sums):
        w = jnp.exp2(log_beta + (res[:, :LANES] + accs[h]))
        valid = valid_of(blk)
        if valid is not None:
            w = jnp.where(valid, w, 0.0)
        ws.append(w.astype(BF16))
        accs[h] = accs[h] + res[:, LANES:]
    outs = {}
    for (h, blk), w in zip(chains, ws):
        d = jnp.dot(w, v_of(h, blk), preferred_element_type=F32)
        outs[h] = d if h not in outs else outs[h] + d
    for h in heads:
        acc_s[h] = accs[h]
        out_s[:, hsl(h)] = out_s[:, hsl(h)] + outs[h]


def _attn_body(qi_ref, kj_ref, bias_ref, q_ref, k_ref, v_ref, u_ref, o_ref, acc_s, out_s, *, tile):
    s = pl.program_id(1)
    qi = qi_ref[s]
    kj = kj_ref[s]

    @pl.when(kj == qi)
    def _():
        acc_s[...] = jnp.zeros_like(acc_s)
        out_s[...] = jnp.zeros_like(out_s)

    u = u_ref[...]

    def sweep(masked):
        hsl = lambda h: slice(h * HEAD_DIM, (h + 1) * HEAD_DIM)
        rsl = lambda sb: slice(sb * LANES, (sb + 1) * LANES)
        chains = [(h, sb) for h in range(N_HEADS) for sb in reversed(range(tile // LANES))]
        if masked:
            col_minus_row = (lax.broadcasted_iota(I32, (tile, LANES), 1)
                             - lax.broadcasted_iota(I32, (tile, LANES), 0))
            valid_of = lambda sb: col_minus_row < -sb * LANES
        else:
            valid_of = lambda sb: None
        _sb_sweep(chains,
                  lambda h: q_ref[:, hsl(h)].astype(BF16),
                  lambda h, sb: k_ref[rsl(sb), hsl(h)].astype(BF16),
                  lambda h, sb: v_ref[rsl(sb), hsl(h)].astype(BF16),
                  valid_of, bias_ref, u, acc_s, out_s)

    @pl.when(kj == qi)
    def _():
        sweep(True)

    @pl.when(kj != qi)
    def _():
        sweep(False)

    @pl.when(kj == 0)
    def _():
        o_ref[...] = out_s[...]


def _attn_prompt(q, k, v, bias):
    nb, s_len, _ = q.shape
    tile = min(ATT_TILE, s_len)
    nq = s_len // tile
    qi = np.concatenate([np.full(i + 1, i) for i in range(nq)]).astype(np.int32)
    kj = np.concatenate([np.arange(i, -1, -1) for i in range(nq)]).astype(np.int32)
    qspec = pl.BlockSpec((None, tile, D_MODEL), lambda b, s, qi_r, kj_r, bias_r: (b, qi_r[s], 0))
    kspec = pl.BlockSpec((None, tile, D_MODEL), lambda b, s, qi_r, kj_r, bias_r: (b, kj_r[s], 0))
    uspec = pl.BlockSpec((2 * LANES, 2 * LANES), lambda b, s, *_: (0, 0))
    return pl.pallas_call(
        functools.partial(_attn_body, tile=tile),
        grid_spec=pltpu.PrefetchScalarGridSpec(
            num_scalar_prefetch=3,
            grid=(nb, len(qi)),
            in_specs=[qspec, kspec, kspec, uspec],
            out_specs=qspec,
            scratch_shapes=[pltpu.VMEM((N_HEADS, tile, LANES), F32), pltpu.VMEM((tile, D_MODEL), F32)],
        ),
        out_shape=jax.ShapeDtypeStruct((nb, s_len, D_MODEL), F32),
        compiler_params=_params("arbitrary", "arbitrary"),
        name="sb_attn_prompt",
    )(jnp.asarray(qi), jnp.asarray(kj), bias, q, k, v, _suffix_matrix())


def _sattn_body(pt_ref, bias_ref, q_ref, kn_ref, vn_ref, kp_ref, vp_ref, u_ref, o_ref,
                acc_s, out_s, kpad, vpad, *, t_new):
    n = pl.program_id(0)
    p = pl.program_id(1)
    u = u_ref[...]
    q_rows = q_ref.shape[0]

    def sweep(k_src, v_src, valid):
        hsl = lambda h: slice(h * HEAD_DIM, (h + 1) * HEAD_DIM)
        head_rows = lambda src, h: src[pl.ds(h, PAGE_SIZE, stride=N_HEADS), :].astype(BF16)
        _sb_sweep([(h, 0) for h in range(N_HEADS)],
                  lambda h: q_ref[:, hsl(h)].astype(BF16),
                  lambda h, blk: head_rows(k_src, h),
                  lambda h, blk: head_rows(v_src, h),
                  lambda blk: valid, bias_ref, u, acc_s, out_s)

    @pl.when((n == 0) & (p == 0))
    def _():
        kpad[...] = jnp.zeros_like(kpad)
        vpad[...] = jnp.zeros_like(vpad)

    @pl.when(p == 0)
    def _():
        kpad[0:t_new * N_HEADS, :] = kn_ref[...]
        vpad[0:t_new * N_HEADS, :] = vn_ref[...]
        acc_s[...] = jnp.zeros_like(acc_s)
        out_s[...] = jnp.zeros_like(out_s)
        valid = (lax.broadcasted_iota(I32, (q_rows, LANES), 1)
                 < lax.broadcasted_iota(I32, (q_rows, LANES), 0))
        sweep(kpad, vpad, valid)

    @pl.when(p > 0)
    def _():
        sweep(kp_ref, vp_ref, None)

    @pl.when(p == pl.num_programs(1) - 1)
    def _():
        o_ref[...] = out_s[...]


def _attn_sample(q8, k_new, v_new, bias, cache_k, cache_v, page_table):
    n, q_rows, _ = q8.shape
    n_pages = page_table.shape[1]
    t_new = k_new.shape[1] // N_HEADS
    page_rows = PAGE_SIZE * N_HEADS

    def page_map(i, p, pt, b):
        logical = n_pages - jnp.maximum(p, 1)
        return (pt[i, logical], 0, 0)

    seq = lambda rows, cols: pl.BlockSpec((None, rows, cols), lambda i, p, pt, b: (i, 0, 0))
    page = pl.BlockSpec((None, page_rows, HEAD_DIM), page_map)
    return pl.pallas_call(
        functools.partial(_sattn_body, t_new=t_new),
        grid_spec=pltpu.PrefetchScalarGridSpec(
            num_scalar_prefetch=2,
            grid=(n, n_pages + 1),
            in_specs=[seq(q_rows, D_MODEL), seq(t_new * N_HEADS, HEAD_DIM), seq(t_new * N_HEADS, HEAD_DIM),
                      page, page, pl.BlockSpec((2 * LANES, 2 * LANES), lambda i, p, pt, b: (0, 0))],
            out_specs=seq(q_rows, D_MODEL),
            scratch_shapes=[pltpu.VMEM((N_HEADS, q_rows, LANES), F32), pltpu.VMEM((q_rows, D_MODEL), F32),
                            pltpu.VMEM((page_rows, HEAD_DIM), F32), pltpu.VMEM((page_rows, HEAD_DIM), F32)],
        ),
        out_shape=jax.ShapeDtypeStruct((n, q_rows, D_MODEL), F32),
        compiler_params=_params("arbitrary", "arbitrary"),
        name="sb_attn_sample",
    )(page_table, bias, q8, k_new, v_new, cache_k, cache_v, _suffix_matrix())


def _outproj_body(x_ref, yl_ref, ya_ref, gl_ref, ga_ref, g1_ref, sh2_ref, sc2_ref, gf_ref,
                  wl_ref, wa_ref, wo_ref, wq_ref, x1_ref, h2_ref, qp_ref):
    yl = jnp.dot(yl_ref[...].astype(BF16), wl_ref[...], preferred_element_type=F32)
    ya = jnp.dot(ya_ref[...].astype(BF16), wa_ref[...], preferred_element_type=F32)
    merged = jax.nn.sigmoid(gl_ref[...]) * yl + jax.nn.sigmoid(ga_ref[...]) * ya
    x1 = x_ref[...] + g1_ref[...] * jnp.dot(merged.astype(BF16), wo_ref[...], preferred_element_type=F32)
    x1_ref[...] = x1
    h2 = _modulated_norm(x1, gf_ref[...], sc2_ref[...], sh2_ref[...])
    h2_ref[...] = h2
    qp_ref[...] = jnp.dot(h2.astype(BF16), wq_ref[...], preferred_element_type=F32)


def _outproj(x, yl, ya, gl, ga, gate1, shift2, scale2, g_ffn, wl_bf, wa_bf, wo_bf, wq_bf, per_seq, rows_per_seq):
    n = x.shape[0]
    tile = min(ROW_TILE, n)
    dq = wq_bf.shape[1]
    row = pl.BlockSpec((tile, D_MODEL), lambda i: (i, 0))
    mod = _mod_spec(per_seq, tile, rows_per_seq)
    return pl.pallas_call(
        _outproj_body,
        grid=(n // tile,),
        in_specs=[row] * 5 + [mod] * 3 + [_resident((1, D_MODEL)), _resident(wl_bf.shape), _resident(wa_bf.shape),
                                          _resident(wo_bf.shape), _resident(wq_bf.shape)],
        out_specs=[row, row, pl.BlockSpec((tile, dq), lambda i: (i, 0))],
        out_shape=[jax.ShapeDtypeStruct((n, D_MODEL), F32), jax.ShapeDtypeStruct((n, D_MODEL), F32),
                   jax.ShapeDtypeStruct((n, dq), F32)],
        compiler_params=_params("arbitrary"),
        name="out_proj",
    )(x, yl, ya, gl, ga, gate1, shift2, scale2, g_ffn, wl_bf, wa_bf, wo_bf, wq_bf)


def _colmax(x):
    return jnp.max(x, axis=0, keepdims=True)


def _colmin(x):
    return jnp.min(x, axis=0, keepdims=True)


def _top16_keys(s):
    t = s.shape[1]
    key = lax.broadcasted_iota(I32, s.shape, 0)
    slot = lax.broadcasted_iota(I32, (PEER_TOPK, t), 0)
    vals = jnp.zeros((PEER_TOPK, t), F32)
    idxs = jnp.zeros((PEER_TOPK, t), I32)
    for r in range(PEER_TOPK):
        m = _colmax(s)
        idx = _colmin(jnp.where(s == m, key, N_KEYS))
        vals = jnp.where(slot == r, m, vals)
        idxs = jnp.where(slot == r, idx, idxs)
        s = jnp.where(key == idx, NEG_INF, s)
    return vals, idxs


def _candidates(v1, i1, v2, i2):
    t = v1.shape[1]
    r8 = lax.broadcasted_iota(I32, (SUBLANES, t), 0)
    r16 = lax.broadcasted_iota(I32, (2 * SUBLANES, t), 0)
    lo, hi = slice(0, SUBLANES), slice(SUBLANES, 2 * SUBLANES)
    row = lambda x, a: x[a:a + 1, :]
    vals, eids, flats = [], [], []

    def add(v, e, f, keep=None):
        vals.append(v if keep is None else jnp.where(keep, v, NEG_INF))
        eids.append(e)
        flats.append(f)

    add(row(v1, 0) + v2, row(i1, 0) * N_KEYS + i2, r16)
    add(row(v1, 1) + v2[lo], row(i1, 1) * N_KEYS + i2[lo], PEER_TOPK + r8)
    for a in (2, 3):
        add(row(v1, a) + v2[lo], row(i1, a) * N_KEYS + i2[lo], a * PEER_TOPK + r8)
    add(v1[hi] + row(v2, 0), i1[hi] * N_KEYS + row(i2, 0), (r8 + SUBLANES) * PEER_TOPK)
    for b in (0, 1, 2):
        add(v1[lo] + row(v2, b), i1[lo] * N_KEYS + row(i2, b), r8 * PEER_TOPK + b, keep=r8 >= 4)
    return (jnp.concatenate(vals, axis=0), jnp.concatenate(eids, axis=0), jnp.concatenate(flats, axis=0))


def _top16_candidates(cand, eid, flat):
    t = cand.shape[1]
    slot = lax.broadcasted_iota(I32, (PEER_TOPK, t), 0)
    best = jnp.zeros((PEER_TOPK, t), F32)
    sel = jnp.zeros((PEER_TOPK, t), I32)
    for r in range(PEER_TOPK):
        m = _colmax(cand)
        f = _colmin(jnp.where(cand == m, flat, PEER_TOPK * PEER_TOPK))
        hit = flat == f
        e = _colmax(jnp.where(hit, eid, -1))
        best = jnp.where(slot == r, m, best)
        sel = jnp.where(slot == r, e, sel)
        cand = jnp.where(hit, NEG_INF, cand)
    return best, sel


def _topk_body(q_ref, k1_ref, k2_ref, addr_ref, shift_ref, gate_ref, e_t, g_t):
    def head(h, carry):
        c0 = pl.multiple_of(h * D_QUERY, D_QUERY)
        q1 = q_ref[:, pl.ds(c0, D_HALF)].astype(BF16)
        q2 = q_ref[:, pl.ds(pl.multiple_of(c0 + D_HALF, D_HALF), D_HALF)].astype(BF16)
        s1 = lax.dot_general(k1_ref[h], q1, _NT, preferred_element_type=F32)
        s2 = lax.dot_general(k2_ref[h], q2, _NT, preferred_element_type=F32)
        v1, i1 = _top16_keys(s1)
        v2, i2 = _top16_keys(s2)
        best, sel = _top16_candidates(*_candidates(v1, i1, v2, i2))
        ex = jnp.exp(best - best[0:1, :])
        r0 = pl.multiple_of(h * PEER_TOPK, PEER_TOPK)
        e_t[pl.ds(r0, PEER_TOPK), :] = sel
        g_t[pl.ds(r0, PEER_TOPK), :] = ex / jnp.sum(ex, axis=0, keepdims=True)
        return carry

    lax.fori_loop(0, PEER_HEADS, head, 0)
    e = e_t[...].T
    addr_ref[...] = (e & (HALF_EXPERTS - 1)) * N_CHUNK
    shift_ref[...] = ((e >> 13) << 4).astype(F32)
    gate_ref[...] = g_t[...].T


def _topk(qp, k1_bf, k2_bf):
    n = qp.shape[0]
    tile = min(TOPK_TILE, n)
    out = pl.BlockSpec((tile, N_PAIRS), lambda i: (i, 0))
    return pl.pallas_call(
        _topk_body,
        grid=(n // tile,),
        in_specs=[pl.BlockSpec((tile, qp.shape[1]), lambda i: (i, 0)), _resident(k1_bf.shape), _resident(k2_bf.shape)],
        out_specs=[out, out, out],
        out_shape=[jax.ShapeDtypeStruct((n, N_PAIRS), I32), jax.ShapeDtypeStruct((n, N_PAIRS), F32),
                   jax.ShapeDtypeStruct((n, N_PAIRS), F32)],
        scratch_shapes=[pltpu.VMEM((N_PAIRS, tile), I32), pltpu.VMEM((N_PAIRS, tile), F32)],
        compiler_params=_params("arbitrary"),
        name="peer_topk",
    )(qp, k1_bf, k2_bf)


def _pack_table(t):
    b = lax.bitcast_convert_type(t.astype(BF16), jnp.uint16).astype(jnp.uint32)
    w = (b[:HALF_EXPERTS] << 16) | b[HALF_EXPERTS:]
    return w.reshape(HALF_EXPERTS * N_CHUNK, LANES)


def _expert_row(tab_ref, addr, shift_ref, k):
    w = tab_ref[pl.ds(pl.multiple_of(addr, N_CHUNK), N_CHUNK), :]
    sh = jnp.broadcast_to(shift_ref[pl.ds(k, 1), :], (N_CHUNK, LANES))
    return lax.bitcast_convert_type((w << lax.bitcast_convert_type(sh, jnp.uint32)) & jnp.uint32(0xFFFF0000), F32)


def _butterfly(parts, index, steps, axis):
    for step in steps:
        m = (index & step) == 0
        parts = [jnp.where(m, a, b) + pltpu.roll(jnp.where(m, b, a), step, axis)
                 for a, b in zip(parts[0::2], parts[1::2])]
    return parts[0]


def _lane_selector(tile):
    k = np.arange(2 * LANES)[:, None] % LANES
    t = np.arange(tile * LANES)[None, :] // LANES
    return jnp.asarray(k == t, dtype=BF16)


def _rows_from_lanes(x, sel_ref, rows_ref, exact_bf16, dtype):
    t = x.shape[0]
    xt = jnp.concatenate([x, jnp.zeros((LANES - t, LANES), F32)], axis=0).T
    hi = xt.astype(BF16)
    if exact_bf16:
        out = jnp.dot(hi, sel_ref[0:LANES, :], preferred_element_type=F32)
    else:
        lo = (xt - hi.astype(F32)).astype(BF16)
        out = jnp.dot(jnp.concatenate([hi, lo], axis=1), sel_ref[...], preferred_element_type=F32)
    for i in range(t):
        rows_ref[i * LANES:(i + 1) * LANES, :] = out[:, i * LANES:(i + 1) * LANES].astype(dtype)


def _row_sums_to_lanes(s, tile):
    hi = s.astype(BF16)
    lo = (s - hi.astype(F32)).astype(BF16)
    r = jnp.dot(jnp.concatenate([hi, lo], axis=1), jnp.ones((2 * LANES, LANES), BF16), preferred_element_type=F32)
    groups = N_PAIRS // SUBLANES
    r = r.reshape(tile, groups, SUBLANES, LANES)
    shape = (groups, SUBLANES, LANES)
    lane = lax.broadcasted_iota(I32, shape, 2)
    keep = ((lane >> 3) == lax.broadcasted_iota(I32, shape, 0)) & ((lane & 7) == lax.broadcasted_iota(I32, shape, 1))
    return jnp.sum(jnp.where(keep[None], r, 0.0), axis=(1, 2))


def _peer_u_body(addr_ref, shift_ref, h_ref, gate_ref, sel_ref, tab_ref, coef_ref, sh_rows, part_s, *, tile):
    _rows_from_lanes(shift_ref[...], sel_ref, sh_rows, True, I32)
    sub = lax.broadcasted_iota(I32, (SUBLANES, LANES), 0)

    def token(t, carry):
        h = h_ref[pl.ds(pl.multiple_of(t * N_CHUNK, N_CHUNK), N_CHUNK), :]
        for g in range(N_PAIRS // SUBLANES):
            prods = []
            for r in range(SUBLANES):
                k = t * N_PAIRS + g * SUBLANES + r
                prods.append(_expert_row(tab_ref, addr_ref[0, 0, k], sh_rows, k) * h)
            row0 = pl.multiple_of(t * N_PAIRS + g * SUBLANES, SUBLANES)
            part_s[pl.ds(row0, SUBLANES), :] = _butterfly(prods, sub, (1, 2, 4), 0)
        return carry

    lax.fori_loop(0, tile, token, 0)
    coef_ref[...] = gate_ref[...] * jax.nn.gelu(_row_sums_to_lanes(part_s[...], tile))


def _peer_v_body(addr_ref, shift_ref, coef_ref, sel_ref, tab_ref, o_ref, sh_rows, coef_rows, *, tile):
    _rows_from_lanes(shift_ref[...], sel_ref, sh_rows, True, I32)
    _rows_from_lanes(coef_ref[...], sel_ref, coef_rows, False, F32)

    def token(t, carry):
        accs = [jnp.zeros((N_CHUNK, LANES), F32) for _ in range(4)]
        for j in range(N_PAIRS):
            k = t * N_PAIRS + j
            c = jnp.broadcast_to(coef_rows[pl.ds(k, 1), :], (N_CHUNK, LANES))
            accs[j % 4] = accs[j % 4] + c * _expert_row(tab_ref, addr_ref[0, 0, k], sh_rows, k)
        o_ref[pl.ds(pl.multiple_of(t * N_CHUNK, N_CHUNK), N_CHUNK), :] = (accs[0] + accs[1]) + (accs[2] + accs[3])
        return carry

    lax.fori_loop(0, tile, token, 0)


def _peer_apply(h2, addr, shift, gate, tab_u, tab_v):
    n = h2.shape[0]
    tile = min(PEER_TILE, n)
    steps = n // tile
    smem = pl.BlockSpec((1, 1, tile * N_PAIRS), lambda i: (i, 0, 0), memory_space=pltpu.SMEM)
    rows = pl.BlockSpec((tile * N_CHUNK, LANES), lambda i: (i, 0))
    pairs = pl.BlockSpec((tile, N_PAIRS), lambda i: (i, 0))
    addr_s = addr.reshape(steps, 1, tile * N_PAIRS)
    rep = lambda dtype: pltpu.VMEM((tile * N_PAIRS, LANES), dtype)
    sel = _lane_selector(tile)
    coef = pl.pallas_call(
        functools.partial(_peer_u_body, tile=tile),
        grid=(steps,),
        in_specs=[smem, pairs, rows, pairs, _resident(sel.shape), _resident(tab_u.shape)],
        out_specs=pairs,
        out_shape=jax.ShapeDtypeStruct((n, N_PAIRS), F32),
        scratch_shapes=[rep(I32), rep(F32)],
        compiler_params=_params("arbitrary"),
        name="peer_score",
    )(addr_s, shift, h2.reshape(n * N_CHUNK, LANES), gate, sel, tab_u)
    out = pl.pallas_call(
        functools.partial(_peer_v_body, tile=tile),
        grid=(steps,),
        in_specs=[smem, pairs, pairs, _resident(sel.shape), _resident(tab_v.shape)],
        out_specs=rows,
        out_shape=jax.ShapeDtypeStruct((n * N_CHUNK, LANES), F32),
        scratch_shapes=[rep(I32), rep(F32)],
        compiler_params=_params("arbitrary"),
        name="peer_mix",
    )(addr_s, shift, coef, sel, tab_v)
    return out.reshape(n, D_MODEL)


def _final_body(x_ref, p_ref, g2_ref, gf_ref, y_ref):
    x = x_ref[...] + g2_ref[...] * p_ref[...]
    ms = jnp.mean(x * x, axis=-1, keepdims=True)
    y_ref[...] = x * lax.rsqrt(ms + EPS) * gf_ref[...]


def _final(x1, peer_out, gate2, g_final, per_seq, rows_per_seq):
    n = x1.shape[0]
    tile = min(ROW_TILE, n)
    row = pl.BlockSpec((tile, D_MODEL), lambda i: (i, 0))
    return pl.pallas_call(
        _final_body,
        grid=(n // tile,),
        in_specs=[row, row, _mod_spec(per_seq, tile, rows_per_seq), _resident((1, D_MODEL))],
        out_specs=row,
        out_shape=jax.ShapeDtypeStruct((n, D_MODEL), F32),
        compiler_params=_params("arbitrary"),
        name="final_norm",
    )(x1, peer_out, gate2, g_final)


def _group_layer(x, mod, per_seq, attend, conv_prev, h0, p):
    nseq, t, _ = x.shape
    n = nseq * t
    xf = x.reshape(n, D_MODEL)
    if per_seq:
        mods = [m.reshape(nseq, 1, D_MODEL) for m in jnp.split(mod, 6, axis=-1)]
    else:
        mods = [jnp.repeat(m, t, axis=0) for m in jnp.split(mod, 6, axis=-1)]
    shift1, scale1, gate1, shift2, scale2, gate2 = mods

    xl, q, k, v, gl, ga = _inproj(xf, shift1, scale1, p["g_mix"], p["w_in"], per_seq, t)

    t_pad = -(-t // SUBLANES) * SUBLANES
    xl3 = jnp.pad(xl.reshape(nseq, t, D_MODEL), ((0, 0), (0, t_pad - t), (0, 0)))
    cprev8 = jnp.pad(conv_prev, ((0, 0), (SUBLANES - (CONV_W - 1), 0), (0, 0)))
    yl, conv8, h_last = _lru(xl3, cprev8, h0.reshape(nseq, 1, D_MODEL), p["conv_w"], p["conv_b"],
                             p["lru_wa"], p["lru_wx"], p["lru_ba"], p["lru_bx"], p["lru_lambda"], t)
    yl = yl[:, :t].reshape(n, D_MODEL)
    conv_new = conv8[:, SUBLANES - (CONV_W - 1):]
    h_new = h_last.reshape(nseq, D_MODEL)

    ya = attend(q, k, v).reshape(n, D_MODEL)

    x1, h2, qp = _outproj(xf, yl, ya, gl, ga, gate1, shift2, scale2, p["g_ffn"], p["w_lru_out"], p["w_att_out"],
                          p["w_o"], p["peer_wq"], per_seq, t)
    addr, shift, gate = _topk(qp, p["peer_k1"], p["peer_k2"])
    peer_out = _peer_apply(h2, addr, shift, gate, p["peer_u"], p["peer_v"])
    y = _final(x1, peer_out, gate2, p["g_final"], per_seq, t)
    kv_shape = (nseq, t, N_HEADS, HEAD_DIM)
    return y.reshape(nseq, t, D_MODEL), k.reshape(kv_shape), v.reshape(kv_shape), conv_new, h_new


def kernel(x_prompt, x_sample, cache_k, cache_v, state_conv, state_lru, page_table, c_prompt, c_sample,
           g_mix, w_ada, b_ada, w_in, conv_w, conv_b, lru_wa, lru_ba, lru_wx, lru_bx, lru_lambda, sb_bias,
           w_lru_out, w_att_out, w_o, g_ffn, peer_wq, peer_k1, peer_k2, peer_u, peer_v, g_final):
    depth = w_in.shape[0]
    nb = x_prompt.shape[0]
    nd, t_dec, _ = x_sample.shape
    n_phys = cache_k.shape[1]
    row = lambda a: a.reshape(1, D_MODEL)
    xp, xs = x_prompt, x_sample
    outs = [[] for _ in range(8)]
    for l in range(depth):
        p = {"g_mix": row(g_mix[l]), "w_in": w_in[l].astype(BF16), "conv_w": conv_w[l], "conv_b": row(conv_b[l]),
             "lru_wa": lru_wa[l].astype(BF16), "lru_wx": lru_wx[l].astype(BF16), "lru_ba": row(lru_ba[l]),
             "lru_bx": row(lru_bx[l]), "lru_lambda": row(lru_lambda[l]), "w_lru_out": w_lru_out[l].astype(BF16),
             "w_att_out": w_att_out[l].astype(BF16), "w_o": w_o[l].astype(BF16), "g_ffn": row(g_ffn[l]),
             "peer_wq": peer_wq[l].astype(BF16), "peer_k1": peer_k1[l].astype(BF16),
             "peer_k2": peer_k2[l].astype(BF16), "peer_u": _pack_table(peer_u[l]), "peer_v": _pack_table(peer_v[l]),
             "g_final": row(g_final)}
        bias = sb_bias[l] * LOG2E
        mod = _ada(jnp.concatenate([c_prompt, c_sample], axis=0), w_ada[l].astype(BF16), b_ada[l].reshape(1, -1))

        def attend_prompt(q, k, v):
            shape = (nb, q.shape[0] // nb, D_MODEL)
            return _attn_prompt(q.reshape(shape), k.reshape(shape), v.reshape(shape), bias)

        def attend_sample(q, k, v, l=l):
            q8 = jnp.pad(q.reshape(nd, t_dec, D_MODEL), ((0, 0), (0, SUBLANES - t_dec), (0, 0)))
            paged = lambda c: c[l].reshape(n_phys, PAGE_SIZE * N_HEADS, HEAD_DIM)
            new = lambda a: a.reshape(nd, t_dec * N_HEADS, HEAD_DIM)
            out = _attn_sample(q8, new(k), new(v), bias, paged(cache_k), paged(cache_v), page_table)
            return out[:, :t_dec]

        zeros_conv = jnp.zeros((nb, CONV_W - 1, D_MODEL), F32)
        zeros_h = jnp.zeros((nb, D_MODEL), F32)
        xp, kp, vp, cp, hp = _group_layer(xp, mod[:nb], True, attend_prompt, zeros_conv, zeros_h, p)
        xs, ks, vs, cs, hs = _group_layer(xs, mod[nb:], False, attend_sample, state_conv[l], state_lru[l], p)
        for lst, val in zip(outs, (kp, vp, cp, hp, ks, vs, cs, hs)):
            lst.append(val)
    assert depth == 1
    return (xp, xs) + tuple(jnp.stack(o) for o in outs)
```

```python
import functools

import numpy as np
import jax
import jax.numpy as jnp
from jax import lax
from jax.experimental import pallas as pl
from jax.experimental.pallas import tpu as pltpu

F32 = jnp.float32
BF16 = jnp.bfloat16
I32 = jnp.int32

LANES = 128
SUBLANES = 8
D_MODEL = 1024
N_CHUNK = D_MODEL // LANES
N_HEADS = 8
HEAD_DIM = 128
LRU_BLOCKS = 8
LRU_BLOCK_W = D_MODEL // LRU_BLOCKS
CONV_W = 4
LRU_C = 8.0
SB_SCALE = HEAD_DIM ** -0.5
LOG2E = 1.4426950408889634
PAGE_SIZE = 128
N_KEYS = 128
N_EXPERTS = N_KEYS * N_KEYS
HALF_EXPERTS = N_EXPERTS // 2
PEER_HEADS = 8
PEER_TOPK = 16
N_PAIRS = PEER_HEADS * PEER_TOPK
D_QUERY = 256
D_HALF = D_QUERY // 2
EPS = 1e-6
NEG_INF = float("-inf")

ROW_TILE = 256
LRU_TILE = 256
ATT_TILE = 256
TOPK_TILE = 256
TOPK_HEADS_PER_ITER = 4
PEER_TILE = 64
SAMPLE_PAGES_PER_STEP = 4
VMEM_LIMIT = 56 * 1024 * 1024

_NT = (((1,), (1,)), ((), ()))


def _params(*sem):
    return pltpu.CompilerParams(dimension_semantics=sem, vmem_limit_bytes=VMEM_LIMIT)


def _resident(shape):
    n = len(shape)
    return pl.BlockSpec(shape, lambda *_: (0,) * n, pipeline_mode=pl.Buffered(1))


def _mod_spec(per_seq, tile, rows_per_seq):
    if per_seq:
        tiles_per_seq = rows_per_seq // tile
        return pl.BlockSpec((None, 1, D_MODEL), lambda i: (i // tiles_per_seq, 0, 0))
    return pl.BlockSpec((tile, D_MODEL), lambda i: (i, 0))


def _ada_body(c_ref, w_ref, b_ref, o_ref):
    c = c_ref[...]
    s = c * jax.nn.sigmoid(c)
    o_ref[...] = jnp.dot(s.astype(BF16), w_ref[...], preferred_element_type=F32) + b_ref[...]


def _ada(c, w_bf, b):
    r = c.shape[0]
    n_out = w_bf.shape[1]
    return pl.pallas_call(
        _ada_body,
        grid=(n_out // D_MODEL,),
        in_specs=[pl.BlockSpec((r, D_MODEL), lambda j: (0, 0)),
                  pl.BlockSpec((D_MODEL, D_MODEL), lambda j: (0, j)),
                  pl.BlockSpec((1, D_MODEL), lambda j: (0, j))],
        out_specs=pl.BlockSpec((r, D_MODEL), lambda j: (0, j)),
        out_shape=jax.ShapeDtypeStruct((r, n_out), F32),
        compiler_params=_params("arbitrary"),
        name="ada_mod",
    )(c, w_bf, b)


def _modulated_norm(x, g, scale, shift):
    ms = jnp.mean(x * x, axis=-1, keepdims=True)
    return (x * lax.rsqrt(ms + EPS) * g) * (1.0 + scale) + shift


def _inproj_body(x_ref, sh_ref, sc_ref, g_ref, w_ref, *out_refs):
    h = _modulated_norm(x_ref[...], g_ref[...], sc_ref[...], sh_ref[...]).astype(BF16)
    for i, o_ref in enumerate(out_refs):
        o_ref[...] = jnp.dot(h, w_ref[:, i * D_MODEL:(i + 1) * D_MODEL], preferred_element_type=F32)


def _inproj(x, shift, scale, g, w_bf, per_seq, rows_per_seq):
    n = x.shape[0]
    tile = min(ROW_TILE, n)
    n_out = w_bf.shape[1] // D_MODEL
    row = pl.BlockSpec((tile, D_MODEL), lambda i: (i, 0))
    mod = _mod_spec(per_seq, tile, rows_per_seq)
    return pl.pallas_call(
        _inproj_body,
        grid=(n // tile,),
        in_specs=[row, mod, mod, _resident((1, D_MODEL)), _resident(w_bf.shape)],
        out_specs=[row] * n_out,
        out_shape=[jax.ShapeDtypeStruct((n, D_MODEL), F32)] * n_out,
        compiler_params=_params("arbitrary"),
        name="in_proj",
    )(x, shift, scale, g, w_bf)


def _lru_body(xl_ref, cprev_ref, h0_ref, cw_ref, cb_ref, wa_ref, wx_ref, ba_ref, bx_ref, lam_ref,
              yl_ref, cnew_ref, hl_ref, xbuf, a_s, b_s, hcar, *, tt, t_last):
    j = pl.program_id(1)

    @pl.when(j == 0)
    def _():
        xbuf[0:SUBLANES, :] = cprev_ref[...]
        hcar[...] = jnp.broadcast_to(h0_ref[...], (SUBLANES, D_MODEL))

    xbuf[SUBLANES:SUBLANES + tt, :] = xl_ref[...]
    xc = cb_ref[...]
    for i in range(CONV_W):
        off = SUBLANES - (CONV_W - 1) + i
        xc = xc + cw_ref[i:i + 1, :] * xbuf[off:off + tt, :]
    xcb = xc.astype(BF16)
    r_parts, i_parts = [], []
    for g in range(LRU_BLOCKS):
        xg = xcb[:, g * LRU_BLOCK_W:(g + 1) * LRU_BLOCK_W]
        r_parts.append(jnp.dot(xg, wa_ref[g], preferred_element_type=F32))
        i_parts.append(jnp.dot(xg, wx_ref[g], preferred_element_type=F32))
    r = jax.nn.sigmoid(jnp.concatenate(r_parts, axis=1) + ba_ref[...])
    ig = jax.nn.sigmoid(jnp.concatenate(i_parts, axis=1) + bx_ref[...])
    nl = -lam_ref[...]
    softplus = jnp.maximum(nl, 0.0) + jnp.log1p(jnp.exp(-jnp.abs(nl)))
    log_a = (-LRU_C) * r * softplus
    a = jnp.exp(log_a)
    a_s[...] = a
    b_s[...] = jnp.sqrt(-jnp.tanh(log_a) * (1.0 + a * a)) * ig * xc

    sub = lax.broadcasted_iota(I32, (SUBLANES, D_MODEL), 0)

    def slab(i, hc):
        r0 = pl.multiple_of(i * SUBLANES, SUBLANES)
        a = a_s[pl.ds(r0, SUBLANES), :]
        b = b_s[pl.ds(r0, SUBLANES), :]
        for s in (1, 2, 4):
            m = sub >= s
            b = jnp.where(m, b + a * pltpu.roll(b, s, 0), b)
            a = jnp.where(m, a * pltpu.roll(a, s, 0), a)
        h = a * hc + b
        yl_ref[pl.ds(r0, SUBLANES), :] = h
        return jnp.broadcast_to(h[SUBLANES - 1:SUBLANES, :], (SUBLANES, D_MODEL))

    hcar[...] = lax.fori_loop(0, tt // SUBLANES, slab, hcar[...])

    @pl.when(j == pl.num_programs(1) - 1)
    def _():
        cnew_ref[...] = xbuf[t_last:t_last + SUBLANES, :]
        hl_ref[...] = yl_ref[t_last - 1:t_last, :]

    xbuf[0:SUBLANES, :] = xbuf[tt:tt + SUBLANES, :]


def _lru(xl, cprev8, h0, cw, cb, wa_bf, wx_bf, ba, bx, lam, t_real):
    nb, t_pad, _ = xl.shape
    tt = min(LRU_TILE, t_pad)
    nt = t_pad // tt
    t_last = t_real - (nt - 1) * tt
    seq = lambda rows: pl.BlockSpec((None, rows, D_MODEL), lambda b, j: (b, 0, 0))
    body = functools.partial(_lru_body, tt=tt, t_last=t_last)
    return pl.pallas_call(
        body,
        grid=(nb, nt),
        in_specs=[pl.BlockSpec((None, tt, D_MODEL), lambda b, j: (b, j, 0)), seq(SUBLANES), seq(1),
                  _resident(cw.shape), _resident(cb.shape), _resident(wa_bf.shape), _resident(wx_bf.shape),
                  _resident(ba.shape), _resident(bx.shape), _resident(lam.shape)],
        out_specs=[pl.BlockSpec((None, tt, D_MODEL), lambda b, j: (b, j, 0)), seq(SUBLANES), seq(1)],
        out_shape=[jax.ShapeDtypeStruct((nb, t_pad, D_MODEL), F32),
                   jax.ShapeDtypeStruct((nb, SUBLANES, D_MODEL), F32),
                   jax.ShapeDtypeStruct((nb, 1, D_MODEL), F32)],
        scratch_shapes=[pltpu.VMEM((tt + SUBLANES, D_MODEL), F32), pltpu.VMEM((tt, D_MODEL), F32),
                        pltpu.VMEM((tt, D_MODEL), F32), pltpu.VMEM((SUBLANES, D_MODEL), F32)],
        compiler_params=_params("arbitrary", "arbitrary"),
        name="rglru",
    )(xl, cprev8, h0, cw, cb, wa_bf, wx_bf, ba, bx, lam)


def _suffix_matrix():
    kp = np.arange(2 * LANES)[:, None] % LANES
    kk = np.arange(2 * LANES)[None, :]
    m = np.where(kk < LANES, kp > kk, True)
    return jnp.asarray(m, dtype=BF16)


def _sb_logs(z, valid):
    nz = -z
    soft = jnp.log2(1.0 + jnp.exp2(jnp.minimum(z, nz)))
    log_beta = jnp.minimum(z, 0.0) - soft
    log_keep = jnp.minimum(nz, 0.0) - soft
    if valid is not None:
        log_keep = jnp.where(valid, log_keep, 0.0)
    hi = log_keep.astype(BF16)
    lo = (log_keep - hi.astype(F32)).astype(BF16)
    return log_beta, jnp.concatenate([hi, lo], axis=1)


def _sb_sweep(chains, q_of, k_of, v_of, valid_of, bias_ref, u, acc_s, out_s):
    hsl = lambda h: slice(h * HEAD_DIM, (h + 1) * HEAD_DIM)
    heads = sorted({h for h, _ in chains})
    qs = {h: q_of(h) for h in heads}
    zs = [lax.dot_general(qs[h], k_of(h, blk), _NT, preferred_element_type=F32) for h, blk in chains]
    log_betas, splits = [], []
    for (h, blk), z in zip(chains, zs):
        log_beta, hi_lo = _sb_logs(z * (SB_SCALE * LOG2E) + bias_ref[h], valid_of(blk))
        log_betas.append(log_beta)
        splits.append(hi_lo)
    sums = [jnp.dot(x, u, preferred_element_type=F32) for x in splits]
    accs = {h: acc_s[h] for h in heads}
    ws = []
    for (h, blk), log_beta, res in zip(chains, log_betas, sums):
        w = jnp.exp2(log_beta + (res[:, :LANES] + accs[h]))
        valid = valid_of(blk)
        if valid is not None:
            w = jnp.where(valid, w, 0.0)
        ws.append(w.astype(BF16))
        accs[h] = accs[h] + res[:, LANES:]
    outs = {}
    for (h, blk), w in zip(chains, ws):
        d = jnp.dot(w, v_of(h, blk), preferred_element_type=F32)
        outs[h] = d if h not in outs else outs[h] + d
    for h in heads:
        acc_s[h] = accs[h]
        out_s[:, hsl(h)] = out_s[:, hsl(h)] + outs[h]


def _attn_body(qi_ref, kj_ref, bias_ref, q_ref, k_ref, v_ref, u_ref, o_ref, acc_s, out_s, *, tile):
    s = pl.program_id(1)
    qi = qi_ref[s]
    kj = kj_ref[s]

    @pl.when(kj == qi)
    def _():
        acc_s[...] = jnp.zeros_like(acc_s)
        out_s[...] = jnp.zeros_like(out_s)

    u = u_ref[...]

    def sweep(masked):
        hsl = lambda h: slice(h * HEAD_DIM, (h + 1) * HEAD_DIM)
        rsl = lambda sb: slice(sb * LANES, (sb + 1) * LANES)
        chains = [(h, sb) for h in range(N_HEADS) for sb in reversed(range(tile // LANES))]
        if masked:
            col_minus_row = (lax.broadcasted_iota(I32, (tile, LANES), 1)
                             - lax.broadcasted_iota(I32, (tile, LANES), 0))
            valid_of = lambda sb: col_minus_row < -sb * LANES
        else:
            valid_of = lambda sb: None
        _sb_sweep(chains,
                  lambda h: q_ref[:, hsl(h)].astype(BF16),
                  lambda h, sb: k_ref[rsl(sb), hsl(h)].astype(BF16),
                  lambda h, sb: v_ref[rsl(sb), hsl(h)].astype(BF16),
                  valid_of, bias_ref, u, acc_s, out_s)

    @pl.when(kj == qi)
    def _():
        sweep(True)

    @pl.when(kj != qi)
    def _():
        sweep(False)

    @pl.when(kj == 0)
    def _():
        o_ref[...] = out_s[...]


def _attn_prompt(q, k, v, bias):
    nb, s_len, _ = q.shape
    tile = min(ATT_TILE, s_len)
    nq = s_len // tile
    qi = np.concatenate([np.full(i + 1, i) for i in range(nq)]).astype(np.int32)
    kj = np.concatenate([np.arange(i, -1, -1) for i in range(nq)]).astype(np.int32)
    qspec = pl.BlockSpec((None, tile, D_MODEL), lambda b, s, qi_r, kj_r, bias_r: (b, qi_r[s], 0))
    kspec = pl.BlockSpec((None, tile, D_MODEL), lambda b, s, qi_r, kj_r, bias_r: (b, kj_r[s], 0))
    uspec = pl.BlockSpec((2 * LANES, 2 * LANES), lambda b, s, *_: (0, 0))
    return pl.pallas_call(
        functools.partial(_attn_body, tile=tile),
        grid_spec=pltpu.PrefetchScalarGridSpec(
            num_scalar_prefetch=3,
            grid=(nb, len(qi)),
            in_specs=[qspec, kspec, kspec, uspec],
            out_specs=qspec,
            scratch_shapes=[pltpu.VMEM((N_HEADS, tile, LANES), F32), pltpu.VMEM((tile, D_MODEL), F32)],
        ),
        out_shape=jax.ShapeDtypeStruct((nb, s_len, D_MODEL), F32),
        compiler_params=_params("arbitrary", "arbitrary"),
        name="sb_attn_prompt",
    )(jnp.asarray(qi), jnp.asarray(kj), bias, q, k, v, _suffix_matrix())


def _sattn_body(pt_ref, bias_ref, q_ref, kn_ref, vn_ref, *rest, t_new, pages_per_step):
    kp_refs = rest[:pages_per_step]
    vp_refs = rest[pages_per_step:2 * pages_per_step]
    u_ref, o_ref, acc_s, out_s, kpad, vpad = rest[2 * pages_per_step:]
    n = pl.program_id(0)
    p = pl.program_id(1)
    u = u_ref[...]
    q_rows = q_ref.shape[0]

    def sweep(k_srcs, v_srcs, valid):
        hsl = lambda h: slice(h * HEAD_DIM, (h + 1) * HEAD_DIM)
        head_rows = lambda src, h: src[pl.ds(h, PAGE_SIZE, stride=N_HEADS), :].astype(BF16)
        _sb_sweep([(h, i) for h in range(N_HEADS) for i in range(len(k_srcs))],
                  lambda h: q_ref[:, hsl(h)].astype(BF16),
                  lambda h, i: head_rows(k_srcs[i], h),
                  lambda h, i: head_rows(v_srcs[i], h),
                  lambda i: valid, bias_ref, u, acc_s, out_s)

    @pl.when((n == 0) & (p == 0))
    def _():
        kpad[...] = jnp.zeros_like(kpad)
        vpad[...] = jnp.zeros_like(vpad)

    @pl.when(p == 0)
    def _():
        kpad[0:t_new * N_HEADS, :] = kn_ref[...]
        vpad[0:t_new * N_HEADS, :] = vn_ref[...]
        acc_s[...] = jnp.zeros_like(acc_s)
        out_s[...] = jnp.zeros_like(out_s)
        valid = (lax.broadcasted_iota(I32, (q_rows, LANES), 1)
                 < lax.broadcasted_iota(I32, (q_rows, LANES), 0))
        sweep([kpad], [vpad], valid)

    @pl.when(p > 0)
    def _():
        sweep(kp_refs, vp_refs, None)

    @pl.when(p == pl.num_programs(1) - 1)
    def _():
        o_ref[...] = out_s[...]


def _attn_sample(q8, k_new, v_new, bias, cache_k, cache_v, page_table):
    n, q_rows, _ = q8.shape
    n_pages = page_table.shape[1]
    t_new = k_new.shape[1] // N_HEADS
    page_rows = PAGE_SIZE * N_HEADS
    pps = SAMPLE_PAGES_PER_STEP if n_pages % SAMPLE_PAGES_PER_STEP == 0 else 1

    def page_spec(i):
        def page_map(s, p, pt, b):
            logical = n_pages - 1 - (jnp.maximum(p, 1) - 1) * pps - i
            return (pt[s, logical], 0, 0)
        return pl.BlockSpec((None, page_rows, HEAD_DIM), page_map)

    seq = lambda rows, cols: pl.BlockSpec((None, rows, cols), lambda s, p, pt, b: (s, 0, 0))
    pages = [page_spec(i) for i in range(pps)]
    return pl.pallas_call(
        functools.partial(_sattn_body, t_new=t_new, pages_per_step=pps),
        grid_spec=pltpu.PrefetchScalarGridSpec(
            num_scalar_prefetch=2,
            grid=(n, n_pages // pps + 1),
            in_specs=[seq(q_rows, D_MODEL), seq(t_new * N_HEADS, HEAD_DIM), seq(t_new * N_HEADS, HEAD_DIM)]
                     + pages + pages + [pl.BlockSpec((2 * LANES, 2 * LANES), lambda s, p, pt, b: (0, 0))],
            out_specs=seq(q_rows, D_MODEL),
            scratch_shapes=[pltpu.VMEM((N_HEADS, q_rows, LANES), F32), pltpu.VMEM((q_rows, D_MODEL), F32),
                            pltpu.VMEM((page_rows, HEAD_DIM), F32), pltpu.VMEM((page_rows, HEAD_DIM), F32)],
        ),
        out_shape=jax.ShapeDtypeStruct((n, q_rows, D_MODEL), F32),
        compiler_params=_params("arbitrary", "arbitrary"),
        name="sb_attn_sample",
    )(page_table, bias, q8, k_new, v_new, *([cache_k] * pps), *([cache_v] * pps), _suffix_matrix())


def _outproj_body(x_ref, yl_ref, ya_ref, gl_ref, ga_ref, g1_ref, sh2_ref, sc2_ref, gf_ref,
                  wl_ref, wa_ref, wo_ref, wq_ref, x1_ref, h2_ref, qp_ref):
    yl = jnp.dot(yl_ref[...].astype(BF16), wl_ref[...], preferred_element_type=F32)
    ya = jnp.dot(ya_ref[...].astype(BF16), wa_ref[...], preferred_element_type=F32)
    merged = jax.nn.sigmoid(gl_ref[...]) * yl + jax.nn.sigmoid(ga_ref[...]) * ya
    x1 = x_ref[...] + g1_ref[...] * jnp.dot(merged.astype(BF16), wo_ref[...], preferred_element_type=F32)
    x1_ref[...] = x1
    h2 = _modulated_norm(x1, gf_ref[...], sc2_ref[...], sh2_ref[...])
    h2_ref[...] = h2
    qp_ref[...] = jnp.dot(h2.astype(BF16), wq_ref[...], preferred_element_type=F32)


def _outproj(x, yl, ya, gl, ga, gate1, shift2, scale2, g_ffn, wl_bf, wa_bf, wo_bf, wq_bf, per_seq, rows_per_seq):
    n = x.shape[0]
    tile = min(ROW_TILE, n)
    dq = wq_bf.shape[1]
    row = pl.BlockSpec((tile, D_MODEL), lambda i: (i, 0))
    mod = _mod_spec(per_seq, tile, rows_per_seq)
    return pl.pallas_call(
        _outproj_body,
        grid=(n // tile,),
        in_specs=[row] * 5 + [mod] * 3 + [_resident((1, D_MODEL)), _resident(wl_bf.shape), _resident(wa_bf.shape),
                                          _resident(wo_bf.shape), _resident(wq_bf.shape)],
        out_specs=[row, row, pl.BlockSpec((tile, dq), lambda i: (i, 0))],
        out_shape=[jax.ShapeDtypeStruct((n, D_MODEL), F32), jax.ShapeDtypeStruct((n, D_MODEL), F32),
                   jax.ShapeDtypeStruct((n, dq), F32)],
        compiler_params=_params("arbitrary"),
        name="out_proj",
    )(x, yl, ya, gl, ga, gate1, shift2, scale2, g_ffn, wl_bf, wa_bf, wo_bf, wq_bf)


def _colmax(x):
    return jnp.max(x, axis=0, keepdims=True)


def _colmin(x):
    return jnp.min(x, axis=0, keepdims=True)


def _top16_keys(s):
    t = s.shape[1]
    key = lax.broadcasted_iota(I32, s.shape, 0)
    slot = lax.broadcasted_iota(I32, (PEER_TOPK, t), 0)
    vals = jnp.zeros((PEER_TOPK, t), F32)
    idxs = jnp.zeros((PEER_TOPK, t), I32)
    for r in range(PEER_TOPK):
        m = _colmax(s)
        idx = _colmin(jnp.where(s == m, key, N_KEYS))
        vals = jnp.where(slot == r, m, vals)
        idxs = jnp.where(slot == r, idx, idxs)
        s = jnp.where(key == idx, NEG_INF, s)
    return vals, idxs


def _candidates(v1, i1, v2, i2):
    t = v1.shape[1]
    r8 = lax.broadcasted_iota(I32, (SUBLANES, t), 0)
    r16 = lax.broadcasted_iota(I32, (2 * SUBLANES, t), 0)
    lo, hi = slice(0, SUBLANES), slice(SUBLANES, 2 * SUBLANES)
    row = lambda x, a: x[a:a + 1, :]
    vals, eids, flats = [], [], []

    def add(v, e, f, keep=None):
        vals.append(v if keep is None else jnp.where(keep, v, NEG_INF))
        eids.append(e)
        flats.append(f)

    add(row(v1, 0) + v2, row(i1, 0) * N_KEYS + i2, r16)
    add(row(v1, 1) + v2[lo], row(i1, 1) * N_KEYS + i2[lo], PEER_TOPK + r8)
    for a in (2, 3):
        add(row(v1, a) + v2[lo], row(i1, a) * N_KEYS + i2[lo], a * PEER_TOPK + r8)
    add(v1[hi] + row(v2, 0), i1[hi] * N_KEYS + row(i2, 0), (r8 + SUBLANES) * PEER_TOPK)
    for b in (0, 1, 2):
        add(v1[lo] + row(v2, b), i1[lo] * N_KEYS + row(i2, b), r8 * PEER_TOPK + b, keep=r8 >= 4)
    return (jnp.concatenate(vals, axis=0), jnp.concatenate(eids, axis=0), jnp.concatenate(flats, axis=0))


def _top16_candidates(cand, eid, flat):
    t = cand.shape[1]
    slot = lax.broadcasted_iota(I32, (PEER_TOPK, t), 0)
    best = jnp.zeros((PEER_TOPK, t), F32)
    sel = jnp.zeros((PEER_TOPK, t), I32)
    for r in range(PEER_TOPK):
        m = _colmax(cand)
        f = _colmin(jnp.where(cand == m, flat, PEER_TOPK * PEER_TOPK))
        hit = flat == f
        e = _colmax(jnp.where(hit, eid, -1))
        best = jnp.where(slot == r, m, best)
        sel = jnp.where(slot == r, e, sel)
        cand = jnp.where(hit, NEG_INF, cand)
    return best, sel


def _topk_body(q_ref, k1_ref, k2_ref, addr_ref, shift_ref, gate_ref, e_t, g_t):
    def head(h, carry):
        c0 = pl.multiple_of(h * D_QUERY, D_QUERY)
        q1 = q_ref[:, pl.ds(c0, D_HALF)].astype(BF16)
        q2 = q_ref[:, pl.ds(pl.multiple_of(c0 + D_HALF, D_HALF), D_HALF)].astype(BF16)
        s1 = lax.dot_general(k1_ref[h], q1, _NT, preferred_element_type=F32)
        s2 = lax.dot_general(k2_ref[h], q2, _NT, preferred_element_type=F32)
        v1, i1 = _top16_keys(s1)
        v2, i2 = _top16_keys(s2)
        best, sel = _top16_candidates(*_candidates(v1, i1, v2, i2))
        ex = jnp.exp(best - best[0:1, :])
        r0 = pl.multiple_of(h * PEER_TOPK, PEER_TOPK)
        e_t[pl.ds(r0, PEER_TOPK), :] = sel
        g_t[pl.ds(r0, PEER_TOPK), :] = ex / jnp.sum(ex, axis=0, keepdims=True)
        return carry

    def head_group(i, carry):
        for j in range(TOPK_HEADS_PER_ITER):
            head(TOPK_HEADS_PER_ITER * i + j, carry)
        return carry

    lax.fori_loop(0, PEER_HEADS // TOPK_HEADS_PER_ITER, head_group, 0)
    e = e_t[...].T
    addr_ref[...] = (e & (HALF_EXPERTS - 1)) * N_CHUNK
    shift_ref[...] = ((e >> 13) << 4).astype(F32)
    gate_ref[...] = g_t[...].T


def _topk(qp, k1_bf, k2_bf):
    n = qp.shape[0]
    tile = min(TOPK_TILE, n)
    out = pl.BlockSpec((tile, N_PAIRS), lambda i: (i, 0))
    return pl.pallas_call(
        _topk_body,
        grid=(n // tile,),
        in_specs=[pl.BlockSpec((tile, qp.shape[1]), lambda i: (i, 0)), _resident(k1_bf.shape), _resident(k2_bf.shape)],
        out_specs=[out, out, out],
        out_shape=[jax.ShapeDtypeStruct((n, N_PAIRS), I32), jax.ShapeDtypeStruct((n, N_PAIRS), F32),
                   jax.ShapeDtypeStruct((n, N_PAIRS), F32)],
        scratch_shapes=[pltpu.VMEM((N_PAIRS, tile), I32), pltpu.VMEM((N_PAIRS, tile), F32)],
        compiler_params=_params("arbitrary"),
        name="peer_topk",
    )(qp, k1_bf, k2_bf)


def _pack_table(t):
    b = lax.bitcast_convert_type(t.astype(BF16), jnp.uint16).astype(jnp.uint32)
    w = (b[:HALF_EXPERTS] << 16) | b[HALF_EXPERTS:]
    return w.reshape(HALF_EXPERTS * N_CHUNK, LANES)


def _expert_row(tab_ref, addr, shift_ref, k):
    w = tab_ref[pl.ds(pl.multiple_of(addr, N_CHUNK), N_CHUNK), :]
    sh = jnp.broadcast_to(shift_ref[pl.ds(k, 1), :], (N_CHUNK, LANES))
    return lax.bitcast_convert_type((w << lax.bitcast_convert_type(sh, jnp.uint32)) & jnp.uint32(0xFFFF0000), F32)


def _butterfly(parts, index, steps, axis):
    for step in steps:
        m = (index & step) == 0
        parts = [jnp.where(m, a, b) + pltpu.roll(jnp.where(m, b, a), step, axis)
                 for a, b in zip(parts[0::2], parts[1::2])]
    return parts[0]


def _lane_selector(tile):
    k = np.arange(2 * LANES)[:, None] % LANES
    t = np.arange(tile * LANES)[None, :] // LANES
    return jnp.asarray(k == t, dtype=BF16)


def _rows_from_lanes(x, sel_ref, rows_ref, exact_bf16, dtype):
    t = x.shape[0]
    xt = jnp.concatenate([x, jnp.zeros((LANES - t, LANES), F32)], axis=0).T
    hi = xt.astype(BF16)
    if exact_bf16:
        out = jnp.dot(hi, sel_ref[0:LANES, :], preferred_element_type=F32)
    else:
        lo = (xt - hi.astype(F32)).astype(BF16)
        out = jnp.dot(jnp.concatenate([hi, lo], axis=1), sel_ref[...], preferred_element_type=F32)
    for i in range(t):
        rows_ref[i * LANES:(i + 1) * LANES, :] = out[:, i * LANES:(i + 1) * LANES].astype(dtype)


def _row_sums_to_lanes(s, tile):
    hi = s.astype(BF16)
    lo = (s - hi.astype(F32)).astype(BF16)
    r = jnp.dot(jnp.concatenate([hi, lo], axis=1), jnp.ones((2 * LANES, LANES), BF16), preferred_element_type=F32)
    groups = N_PAIRS // SUBLANES
    r = r.reshape(tile, groups, SUBLANES, LANES)
    shape = (groups, SUBLANES, LANES)
    lane = lax.broadcasted_iota(I32, shape, 2)
    keep = ((lane >> 3) == lax.broadcasted_iota(I32, shape, 0)) & ((lane & 7) == lax.broadcasted_iota(I32, shape, 1))
    return jnp.sum(jnp.where(keep[None], r, 0.0), axis=(1, 2))


def _peer_u_body(addr_ref, shift_ref, h_ref, gate_ref, sel_ref, tab_ref, coef_ref, sh_rows, part_s, *, tile):
    _rows_from_lanes(shift_ref[...], sel_ref, sh_rows, True, I32)
    sub = lax.broadcasted_iota(I32, (SUBLANES, LANES), 0)

    def token(t, carry):
        h = h_ref[pl.ds(pl.multiple_of(t * N_CHUNK, N_CHUNK), N_CHUNK), :]
        for g in range(N_PAIRS // SUBLANES):
            prods = []
            for r in range(SUBLANES):
                k = t * N_PAIRS + g * SUBLANES + r
                prods.append(_expert_row(tab_ref, addr_ref[0, 0, k], sh_rows, k) * h)
            row0 = pl.multiple_of(t * N_PAIRS + g * SUBLANES, SUBLANES)
            part_s[pl.ds(row0, SUBLANES), :] = _butterfly(prods, sub, (1, 2, 4), 0)
        return carry

    lax.fori_loop(0, tile, token, 0)
    coef_ref[...] = gate_ref[...] * jax.nn.gelu(_row_sums_to_lanes(part_s[...], tile))


def _peer_v_body(addr_ref, shift_ref, coef_ref, sel_ref, tab_ref, o_ref, sh_rows, coef_rows, *, tile):
    _rows_from_lanes(shift_ref[...], sel_ref, sh_rows, True, I32)
    _rows_from_lanes(coef_ref[...], sel_ref, coef_rows, False, F32)

    def token(t, carry):
        accs = [jnp.zeros((N_CHUNK, LANES), F32) for _ in range(4)]
        for j in range(N_PAIRS):
            k = t * N_PAIRS + j
            c = jnp.broadcast_to(coef_rows[pl.ds(k, 1), :], (N_CHUNK, LANES))
            accs[j % 4] = accs[j % 4] + c * _expert_row(tab_ref, addr_ref[0, 0, k], sh_rows, k)
        o_ref[pl.ds(pl.multiple_of(t * N_CHUNK, N_CHUNK), N_CHUNK), :] = (accs[0] + accs[1]) + (accs[2] + accs[3])
        return carry

    lax.fori_loop(0, tile, token, 0)


def _peer_apply(h2, addr, shift, gate, tab_u, tab_v):
    n = h2.shape[0]
    tile = min(PEER_TILE, n)
    steps = n // tile
    smem = pl.BlockSpec((1, 1, tile * N_PAIRS), lambda i: (i, 0, 0), memory_space=pltpu.SMEM)
    rows = pl.BlockSpec((tile * N_CHUNK, LANES), lambda i: (i, 0))
    pairs = pl.BlockSpec((tile, N_PAIRS), lambda i: (i, 0))
    addr_s = addr.reshape(steps, 1, tile * N_PAIRS)
    rep = lambda dtype: pltpu.VMEM((tile * N_PAIRS, LANES), dtype)
    sel = _lane_selector(tile)
    coef = pl.pallas_call(
        functools.partial(_peer_u_body, tile=tile),
        grid=(steps,),
        in_specs=[smem, pairs, rows, pairs, _resident(sel.shape), _resident(tab_u.shape)],
        out_specs=pairs,
        out_shape=jax.ShapeDtypeStruct((n, N_PAIRS), F32),
        scratch_shapes=[rep(I32), rep(F32)],
        compiler_params=_params("arbitrary"),
        name="peer_score",
    )(addr_s, shift, h2.reshape(n * N_CHUNK, LANES), gate, sel, tab_u)
    out = pl.pallas_call(
        functools.partial(_peer_v_body, tile=tile),
        grid=(steps,),
        in_specs=[smem, pairs, pairs, _resident(sel.shape), _resident(tab_v.shape)],
        out_specs=rows,
        out_shape=jax.ShapeDtypeStruct((n * N_CHUNK, LANES), F32),
        scratch_shapes=[rep(I32), rep(F32)],
        compiler_params=_params("arbitrary"),
        name="peer_mix",
    )(addr_s, shift, coef, sel, tab_v)
    return out.reshape(n, D_MODEL)


def _final_body(x_ref, p_ref, g2_ref, gf_ref, y_ref):
    x = x_ref[...] + g2_ref[...] * p_ref[...]
    ms = jnp.mean(x * x, axis=-1, keepdims=True)
    y_ref[...] = x * lax.rsqrt(ms + EPS) * gf_ref[...]


def _final(x1, peer_out, gate2, g_final, per_seq, rows_per_seq):
    n = x1.shape[0]
    tile = min(ROW_TILE, n)
    row = pl.BlockSpec((tile, D_MODEL), lambda i: (i, 0))
    return pl.pallas_call(
        _final_body,
        grid=(n // tile,),
        in_specs=[row, row, _mod_spec(per_seq, tile, rows_per_seq), _resident((1, D_MODEL))],
        out_specs=row,
        out_shape=jax.ShapeDtypeStruct((n, D_MODEL), F32),
        compiler_params=_params("arbitrary"),
        name="final_norm",
    )(x1, peer_out, gate2, g_final)


def _group_layer(x, mod, per_seq, attend, conv_prev, h0, p):
    nseq, t, _ = x.shape
    n = nseq * t
    xf = x.reshape(n, D_MODEL)
    if per_seq:
        mods = [m.reshape(nseq, 1, D_MODEL) for m in jnp.split(mod, 6, axis=-1)]
    else:
        mods = [jnp.repeat(m, t, axis=0) for m in jnp.split(mod, 6, axis=-1)]
    shift1, scale1, gate1, shift2, scale2, gate2 = mods

    xl, q, k, v, gl, ga = _inproj(xf, shift1, scale1, p["g_mix"], p["w_in"], per_seq, t)

    t_pad = -(-t // SUBLANES) * SUBLANES
    xl3 = jnp.pad(xl.reshape(nseq, t, D_MODEL), ((0, 0), (0, t_pad - t), (0, 0)))
    cprev8 = jnp.pad(conv_prev, ((0, 0), (SUBLANES - (CONV_W - 1), 0), (0, 0)))
    yl, conv8, h_last = _lru(xl3, cprev8, h0.reshape(nseq, 1, D_MODEL), p["conv_w"], p["conv_b"],
                             p["lru_wa"], p["lru_wx"], p["lru_ba"], p["lru_bx"], p["lru_lambda"], t)
    yl = yl[:, :t].reshape(n, D_MODEL)
    conv_new = conv8[:, SUBLANES - (CONV_W - 1):]
    h_new = h_last.reshape(nseq, D_MODEL)

    ya = attend(q, k, v).reshape(n, D_MODEL)

    x1, h2, qp = _outproj(xf, yl, ya, gl, ga, gate1, shift2, scale2, p["g_ffn"], p["w_lru_out"], p["w_att_out"],
                          p["w_o"], p["peer_wq"], per_seq, t)
    addr, shift, gate = _topk(qp, p["peer_k1"], p["peer_k2"])
    peer_out = _peer_apply(h2, addr, shift, gate, p["peer_u"], p["peer_v"])
    y = _final(x1, peer_out, gate2, p["g_final"], per_seq, t)
    kv_shape = (nseq, t, N_HEADS, HEAD_DIM)
    return y.reshape(nseq, t, D_MODEL), k.reshape(kv_shape), v.reshape(kv_shape), conv_new, h_new


def kernel(x_prompt, x_sample, cache_k, cache_v, state_conv, state_lru, page_table, c_prompt, c_sample,
           g_mix, w_ada, b_ada, w_in, conv_w, conv_b, lru_wa, lru_ba, lru_wx, lru_bx, lru_lambda, sb_bias,
           w_lru_out, w_att_out, w_o, g_ffn, peer_wq, peer_k1, peer_k2, peer_u, peer_v, g_final):
    depth = w_in.shape[0]
    nb = x_prompt.shape[0]
    nd, t_dec, _ = x_sample.shape
    n_phys = cache_k.shape[1]
    row = lambda a: a.reshape(1, D_MODEL)
    xp, xs = x_prompt, x_sample
    outs = [[] for _ in range(8)]
    for l in range(depth):
        p = {"g_mix": row(g_mix[l]), "w_in": w_in[l].astype(BF16), "conv_w": conv_w[l], "conv_b": row(conv_b[l]),
             "lru_wa": lru_wa[l].astype(BF16), "lru_wx": lru_wx[l].astype(BF16), "lru_ba": row(lru_ba[l]),
             "lru_bx": row(lru_bx[l]), "lru_lambda": row(lru_lambda[l]), "w_lru_out": w_lru_out[l].astype(BF16),
             "w_att_out": w_att_out[l].astype(BF16), "w_o": w_o[l].astype(BF16), "g_ffn": row(g_ffn[l]),
             "peer_wq": peer_wq[l].astype(BF16), "peer_k1": peer_k1[l].astype(BF16),
             "peer_k2": peer_k2[l].astype(BF16), "peer_u": _pack_table(peer_u[l]), "peer_v": _pack_table(peer_v[l]),
             "g_final": row(g_final)}
        bias = sb_bias[l] * LOG2E
        mod = _ada(jnp.concatenate([c_prompt, c_sample], axis=0), w_ada[l].astype(BF16), b_ada[l].reshape(1, -1))

        def attend_prompt(q, k, v):
            shape = (nb, q.shape[0] // nb, D_MODEL)
            return _attn_prompt(q.reshape(shape), k.reshape(shape), v.reshape(shape), bias)

        def attend_sample(q, k, v, l=l):
            q8 = jnp.pad(q.reshape(nd, t_dec, D_MODEL), ((0, 0), (0, SUBLANES - t_dec), (0, 0)))
            paged = lambda c: c[l].reshape(n_phys, PAGE_SIZE * N_HEADS, HEAD_DIM)
            new = lambda a: a.reshape(nd, t_dec * N_HEADS, HEAD_DIM)
            out = _attn_sample(q8, new(k), new(v), bias, paged(cache_k), paged(cache_v), page_table)
            return out[:, :t_dec]

        zeros_conv = jnp.zeros((nb, CONV_W - 1, D_MODEL), F32)
        zeros_h = jnp.zeros((nb, D_MODEL), F32)
        xp, kp, vp, cp, hp = _group_layer(xp, mod[:nb], True, attend_prompt, zeros_conv, zeros_h, p)
        xs, ks, vs, cs, hs = _group_layer(xs, mod[nb:], False, attend_sample, state_conv[l], state_lru[l], p)
        for lst, val in zip(outs, (kp, vp, cp, hp, ks, vs, cs, hs)):
            lst.append(val)
    assert depth == 1
    return (xp, xs) + tuple(jnp.stack(o) for o in outs)
```

```python
import functools

import numpy as np
import jax
import jax.numpy as jnp
from jax import lax
from jax.experimental import pallas as pl
from jax.experimental.pallas import tpu as pltpu

F32 = jnp.float32
BF16 = jnp.bfloat16
I32 = jnp.int32

LANES = 128
SUBLANES = 8
D_MODEL = 1024
N_CHUNK = D_MODEL // LANES
N_HEADS = 8
HEAD_DIM = 128
LRU_BLOCKS = 8
LRU_BLOCK_W = D_MODEL // LRU_BLOCKS
CONV_W = 4
LRU_C = 8.0
SB_SCALE = HEAD_DIM ** -0.5
LOG2E = 1.4426950408889634
PAGE_SIZE = 128
N_KEYS = 128
N_EXPERTS = N_KEYS * N_KEYS
HALF_EXPERTS = N_EXPERTS // 2
PEER_HEADS = 8
PEER_TOPK = 16
N_PAIRS = PEER_HEADS * PEER_TOPK
D_QUERY = 256
D_HALF = D_QUERY // 2
EPS = 1e-6
NEG_INF = float("-inf")

ROW_TILE = 256
LRU_TILE = 256
ATT_TILE = 256
TOPK_TILE = 256
TOPK_HEADS_PER_ITER = 4
PEER_TILE = 64
SAMPLE_PAGES_PER_STEP = 4
VMEM_LIMIT = 56 * 1024 * 1024

_NT = (((1,), (1,)), ((), ()))


def _params(*sem):
    return pltpu.CompilerParams(dimension_semantics=sem, vmem_limit_bytes=VMEM_LIMIT)


def _resident(shape):
    n = len(shape)
    return pl.BlockSpec(shape, lambda *_: (0,) * n, pipeline_mode=pl.Buffered(1))


def _mod_spec(per_seq, tile, rows_per_seq):
    if per_seq:
        tiles_per_seq = rows_per_seq // tile
        return pl.BlockSpec((None, 1, D_MODEL), lambda i: (i // tiles_per_seq, 0, 0))
    return pl.BlockSpec((tile, D_MODEL), lambda i: (i, 0))


def _ada_body(c_ref, w_ref, b_ref, o_ref):
    c = c_ref[...]
    s = c * jax.nn.sigmoid(c)
    o_ref[...] = jnp.dot(s.astype(BF16), w_ref[...], preferred_element_type=F32) + b_ref[...]


def _ada(c, w_bf, b):
    r = c.shape[0]
    n_out = w_bf.shape[1]
    return pl.pallas_call(
        _ada_body,
        grid=(n_out // D_MODEL,),
        in_specs=[pl.BlockSpec((r, D_MODEL), lambda j: (0, 0)),
                  pl.BlockSpec((D_MODEL, D_MODEL), lambda j: (0, j)),
                  pl.BlockSpec((1, D_MODEL), lambda j: (0, j))],
        out_specs=pl.BlockSpec((r, D_MODEL), lambda j: (0, j)),
        out_shape=jax.ShapeDtypeStruct((r, n_out), F32),
        compiler_params=_params("arbitrary"),
        name="ada_mod",
    )(c, w_bf, b)


def _modulated_norm(x, g, scale, shift):
    ms = jnp.mean(x * x, axis=-1, keepdims=True)
    return (x * lax.rsqrt(ms + EPS) * g) * (1.0 + scale) + shift


def _inproj_body(x_ref, sh_ref, sc_ref, g_ref, w_ref, *out_refs):
    h = _modulated_norm(x_ref[...], g_ref[...], sc_ref[...], sh_ref[...]).astype(BF16)
    for i, o_ref in enumerate(out_refs):
        o_ref[...] = jnp.dot(h, w_ref[:, i * D_MODEL:(i + 1) * D_MODEL], preferred_element_type=F32)


def _inproj(x, shift, scale, g, w_bf, per_seq, rows_per_seq):
    n = x.shape[0]
    tile = min(ROW_TILE, n)
    n_out = w_bf.shape[1] // D_MODEL
    row = pl.BlockSpec((tile, D_MODEL), lambda i: (i, 0))
    mod = _mod_spec(per_seq, tile, rows_per_seq)
    return pl.pallas_call(
        _inproj_body,
        grid=(n // tile,),
        in_specs=[row, mod, mod, _resident((1, D_MODEL)), _resident(w_bf.shape)],
        out_specs=[row] * n_out,
        out_shape=[jax.ShapeDtypeStruct((n, D_MODEL), F32)] * n_out,
        compiler_params=_params("arbitrary"),
        name="in_proj",
    )(x, shift, scale, g, w_bf)


def _lru_body(xl_ref, cprev_ref, h0_ref, cw_ref, cb_ref, wa_ref, wx_ref, ba_ref, bx_ref, lam_ref,
              yl_ref, cnew_ref, hl_ref, xbuf, a_s, b_s, hcar, *, tt, t_last):
    j = pl.program_id(1)

    @pl.when(j == 0)
    def _():
        xbuf[0:SUBLANES, :] = cprev_ref[...]
        hcar[...] = jnp.broadcast_to(h0_ref[...], (SUBLANES, D_MODEL))

    xbuf[SUBLANES:SUBLANES + tt, :] = xl_ref[...]
    xc = cb_ref[...]
    for i in range(CONV_W):
        off = SUBLANES - (CONV_W - 1) + i
        xc = xc + cw_ref[i:i + 1, :] * xbuf[off:off + tt, :]
    xcb = xc.astype(BF16)
    r_parts, i_parts = [], []
    for g in range(LRU_BLOCKS):
        xg = xcb[:, g * LRU_BLOCK_W:(g + 1) * LRU_BLOCK_W]
        r_parts.append(jnp.dot(xg, wa_ref[g], preferred_element_type=F32))
        i_parts.append(jnp.dot(xg, wx_ref[g], preferred_element_type=F32))
    r = jax.nn.sigmoid(jnp.concatenate(r_parts, axis=1) + ba_ref[...])
    ig = jax.nn.sigmoid(jnp.concatenate(i_parts, axis=1) + bx_ref[...])
    nl = -lam_ref[...]
    softplus = jnp.maximum(nl, 0.0) + jnp.log1p(jnp.exp(-jnp.abs(nl)))
    log_a = (-LRU_C) * r * softplus
    a = jnp.exp(log_a)
    a_s[...] = a
    b_s[...] = jnp.sqrt(-jnp.tanh(log_a) * (1.0 + a * a)) * ig * xc

    sub = lax.broadcasted_iota(I32, (SUBLANES, D_MODEL), 0)

    def slab(i, hc):
        r0 = pl.multiple_of(i * SUBLANES, SUBLANES)
        a = a_s[pl.ds(r0, SUBLANES), :]
        b = b_s[pl.ds(r0, SUBLANES), :]
        for s in (1, 2, 4):
            m = sub >= s
            b = jnp.where(m, b + a * pltpu.roll(b, s, 0), b)
            a = jnp.where(m, a * pltpu.roll(a, s, 0), a)
        h = a * hc + b
        yl_ref[pl.ds(r0, SUBLANES), :] = h
        return jnp.broadcast_to(h[SUBLANES - 1:SUBLANES, :], (SUBLANES, D_MODEL))

    hcar[...] = lax.fori_loop(0, tt // SUBLANES, slab, hcar[...])

    @pl.when(j == pl.num_programs(1) - 1)
    def _():
        cnew_ref[...] = xbuf[t_last:t_last + SUBLANES, :]
        hl_ref[...] = yl_ref[t_last - 1:t_last, :]

    xbuf[0:SUBLANES, :] = xbuf[tt:tt + SUBLANES, :]


def _lru(xl, cprev8, h0, cw, cb, wa_bf, wx_bf, ba, bx, lam, t_real):
    nb, t_pad, _ = xl.shape
    tt = min(LRU_TILE, t_pad)
    nt = t_pad // tt
    t_last = t_real - (nt - 1) * tt
    seq = lambda rows: pl.BlockSpec((None, rows, D_MODEL), lambda b, j: (b, 0, 0))
    body = functools.partial(_lru_body, tt=tt, t_last=t_last)
    return pl.pallas_call(
        body,
        grid=(nb, nt),
        in_specs=[pl.BlockSpec((None, tt, D_MODEL), lambda b, j: (b, j, 0)), seq(SUBLANES), seq(1),
                  _resident(cw.shape), _resident(cb.shape), _resident(wa_bf.shape), _resident(wx_bf.shape),
                  _resident(ba.shape), _resident(bx.shape), _resident(lam.shape)],
        out_specs=[pl.BlockSpec((None, tt, D_MODEL), lambda b, j: (b, j, 0)), seq(SUBLANES), seq(1)],
        out_shape=[jax.ShapeDtypeStruct((nb, t_pad, D_MODEL), F32),
                   jax.ShapeDtypeStruct((nb, SUBLANES, D_MODEL), F32),
                   jax.ShapeDtypeStruct((nb, 1, D_MODEL), F32)],
        scratch_shapes=[pltpu.VMEM((tt + SUBLANES, D_MODEL), F32), pltpu.VMEM((tt, D_MODEL), F32),
                        pltpu.VMEM((tt, D_MODEL), F32), pltpu.VMEM((SUBLANES, D_MODEL), F32)],
        compiler_params=_params("arbitrary", "arbitrary"),
        name="rglru",
    )(xl, cprev8, h0, cw, cb, wa_bf, wx_bf, ba, bx, lam)


def _suffix_matrix():
    kp = np.arange(2 * LANES)[:, None] % LANES
    kk = np.arange(2 * LANES)[None, :]
    m = np.where(kk < LANES, kp > kk, True)
    return jnp.asarray(m, dtype=BF16)


def _sb_logs(z, valid):
    nz = -z
    soft = jnp.log2(1.0 + jnp.exp2(jnp.minimum(z, nz)))
    log_beta = jnp.minimum(z, 0.0) - soft
    log_keep = jnp.minimum(nz, 0.0) - soft
    if valid is not None:
        log_keep = jnp.where(valid, log_keep, 0.0)
    return log_beta, log_keep


def _sb_sweep(chains, q_of, k_of, v_of, valid_of, bias_ref, u, acc_s, out_s, stack):
    hsl = lambda h: slice(h * HEAD_DIM, (h + 1) * HEAD_DIM)
    heads = sorted({h for h, _ in chains})
    qs = {h: q_of(h) for h in heads}
    rows = qs[heads[0]].shape[0]
    zs = [lax.dot_general(qs[h], k_of(h, blk), _NT, preferred_element_type=F32) for h, blk in chains]
    log_betas, log_keeps = [], []
    for (h, blk), z in zip(chains, zs):
        log_beta, log_keep = _sb_logs(z * (SB_SCALE * LOG2E) + bias_ref[h], valid_of(blk))
        log_betas.append(log_beta)
        log_keeps.append(log_keep)
    sums = []
    for i in range(0, len(chains), stack):
        stacked = jnp.concatenate(log_keeps[i:i + stack], axis=0)
        hi = stacked.astype(BF16)
        lo = (stacked - hi.astype(F32)).astype(BF16)
        sums.append(jnp.dot(jnp.concatenate([hi, lo], axis=1), u, preferred_element_type=F32))
    accs = {h: acc_s[h] for h in heads}
    ws = []
    for i, ((h, blk), log_beta) in enumerate(zip(chains, log_betas)):
        res = sums[i // stack][(i % stack) * rows:(i % stack + 1) * rows]
        w = jnp.exp2(log_beta + (res[:, :LANES] + accs[h]))
        valid = valid_of(blk)
        if valid is not None:
            w = jnp.where(valid, w, 0.0)
        ws.append(w.astype(BF16))
        accs[h] = accs[h] + res[:, LANES:]
    outs = {}
    for (h, blk), w in zip(chains, ws):
        d = jnp.dot(w, v_of(h, blk), preferred_element_type=F32)
        outs[h] = d if h not in outs else outs[h] + d
    for h in heads:
        acc_s[h] = accs[h]
        out_s[:, hsl(h)] = out_s[:, hsl(h)] + outs[h]


def _attn_body(qi_ref, kj_ref, bias_ref, q_ref, k_ref, v_ref, u_ref, o_ref, acc_s, out_s, *, tile):
    s = pl.program_id(1)
    qi = qi_ref[s]
    kj = kj_ref[s]

    @pl.when(kj == qi)
    def _():
        acc_s[...] = jnp.zeros_like(acc_s)
        out_s[...] = jnp.zeros_like(out_s)

    u = u_ref[...]

    def sweep(masked):
        hsl = lambda h: slice(h * HEAD_DIM, (h + 1) * HEAD_DIM)
        rsl = lambda sb: slice(sb * LANES, (sb + 1) * LANES)
        chains = [(h, sb) for h in range(N_HEADS) for sb in reversed(range(tile // LANES))]
        if masked:
            col_minus_row = (lax.broadcasted_iota(I32, (tile, LANES), 1)
                             - lax.broadcasted_iota(I32, (tile, LANES), 0))
            valid_of = lambda sb: col_minus_row < -sb * LANES
        else:
            valid_of = lambda sb: None
        _sb_sweep(chains,
                  lambda h: q_ref[:, hsl(h)].astype(BF16),
                  lambda h, sb: k_ref[rsl(sb), hsl(h)].astype(BF16),
                  lambda h, sb: v_ref[rsl(sb), hsl(h)].astype(BF16),
                  valid_of, bias_ref, u, acc_s, out_s, stack=1)

    @pl.when(kj == qi)
    def _():
        sweep(True)

    @pl.when(kj != qi)
    def _():
        sweep(False)

    @pl.when(kj == 0)
    def _():
        o_ref[...] = out_s[...]


def _attn_prompt(q, k, v, bias):
    nb, s_len, _ = q.shape
    tile = min(ATT_TILE, s_len)
    nq = s_len // tile
    qi = np.concatenate([np.full(i + 1, i) for i in range(nq)]).astype(np.int32)
    kj = np.concatenate([np.arange(i, -1, -1) for i in range(nq)]).astype(np.int32)
    qspec = pl.BlockSpec((None, tile, D_MODEL), lambda b, s, qi_r, kj_r, bias_r: (b, qi_r[s], 0))
    kspec = pl.BlockSpec((None, tile, D_MODEL), lambda b, s, qi_r, kj_r, bias_r: (b, kj_r[s], 0))
    uspec = pl.BlockSpec((2 * LANES, 2 * LANES), lambda b, s, *_: (0, 0))
    return pl.pallas_call(
        functools.partial(_attn_body, tile=tile),
        grid_spec=pltpu.PrefetchScalarGridSpec(
            num_scalar_prefetch=3,
            grid=(nb, len(qi)),
            in_specs=[qspec, kspec, kspec, uspec],
            out_specs=qspec,
            scratch_shapes=[pltpu.VMEM((N_HEADS, tile, LANES), F32), pltpu.VMEM((tile, D_MODEL), F32)],
        ),
        out_shape=jax.ShapeDtypeStruct((nb, s_len, D_MODEL), F32),
        compiler_params=_params("arbitrary", "arbitrary"),
        name="sb_attn_prompt",
    )(jnp.asarray(qi), jnp.asarray(kj), bias, q, k, v, _suffix_matrix())


def _sattn_body(pt_ref, bias_ref, q_ref, kn_ref, vn_ref, *rest, t_new, pages_per_step):
    kp_refs = rest[:pages_per_step]
    vp_refs = rest[pages_per_step:2 * pages_per_step]
    u_ref, o_ref, acc_s, out_s, kpad, vpad = rest[2 * pages_per_step:]
    n = pl.program_id(0)
    p = pl.program_id(1)
    u = u_ref[...]
    q_rows = q_ref.shape[0]

    def sweep(k_srcs, v_srcs, valid):
        hsl = lambda h: slice(h * HEAD_DIM, (h + 1) * HEAD_DIM)
        head_rows = lambda src, h: src[pl.ds(h, PAGE_SIZE, stride=N_HEADS), :].astype(BF16)
        _sb_sweep([(h, i) for h in range(N_HEADS) for i in range(len(k_srcs))],
                  lambda h: q_ref[:, hsl(h)].astype(BF16),
                  lambda h, i: head_rows(k_srcs[i], h),
                  lambda h, i: head_rows(v_srcs[i], h),
                  lambda i: valid, bias_ref, u, acc_s, out_s, stack=N_HEADS * len(k_srcs))

    @pl.when((n == 0) & (p == 0))
    def _():
        kpad[...] = jnp.zeros_like(kpad)
        vpad[...] = jnp.zeros_like(vpad)

    @pl.when(p == 0)
    def _():
        kpad[0:t_new * N_HEADS, :] = kn_ref[...]
        vpad[0:t_new * N_HEADS, :] = vn_ref[...]
        acc_s[...] = jnp.zeros_like(acc_s)
        out_s[...] = jnp.zeros_like(out_s)
        valid = (lax.broadcasted_iota(I32, (q_rows, LANES), 1)
                 < lax.broadcasted_iota(I32, (q_rows, LANES), 0))
        sweep([kpad], [vpad], valid)

    @pl.when(p > 0)
    def _():
        sweep(kp_refs, vp_refs, None)

    @pl.when(p == pl.num_programs(1) - 1)
    def _():
        o_ref[...] = out_s[...]


def _attn_sample(q8, k_new, v_new, bias, cache_k, cache_v, page_table):
    n, q_rows, _ = q8.shape
    n_pages = page_table.shape[1]
    t_new = k_new.shape[1] // N_HEADS
    page_rows = PAGE_SIZE * N_HEADS
    pps = SAMPLE_PAGES_PER_STEP if n_pages % SAMPLE_PAGES_PER_STEP == 0 else 1

    def page_spec(i):
        def page_map(s, p, pt, b):
            logical = n_pages - 1 - (jnp.maximum(p, 1) - 1) * pps - i
            return (pt[s, logical], 0, 0)
        return pl.BlockSpec((None, page_rows, HEAD_DIM), page_map)

    seq = lambda rows, cols: pl.BlockSpec((None, rows, cols), lambda s, p, pt, b: (s, 0, 0))
    pages = [page_spec(i) for i in range(pps)]
    return pl.pallas_call(
        functools.partial(_sattn_body, t_new=t_new, pages_per_step=pps),
        grid_spec=pltpu.PrefetchScalarGridSpec(
            num_scalar_prefetch=2,
            grid=(n, n_pages // pps + 1),
            in_specs=[seq(q_rows, D_MODEL), seq(t_new * N_HEADS, HEAD_DIM), seq(t_new * N_HEADS, HEAD_DIM)]
                     + pages + pages + [pl.BlockSpec((2 * LANES, 2 * LANES), lambda s, p, pt, b: (0, 0))],
            out_specs=seq(q_rows, D_MODEL),
            scratch_shapes=[pltpu.VMEM((N_HEADS, q_rows, LANES), F32), pltpu.VMEM((q_rows, D_MODEL), F32),
                            pltpu.VMEM((page_rows, HEAD_DIM), F32), pltpu.VMEM((page_rows, HEAD_DIM), F32)],
        ),
        out_shape=jax.ShapeDtypeStruct((n, q_rows, D_MODEL), F32),
        compiler_params=_params("arbitrary", "arbitrary"),
        name="sb_attn_sample",
    )(page_table, bias, q8, k_new, v_new, *([cache_k] * pps), *([cache_v] * pps), _suffix_matrix())


def _outproj_body(x_ref, yl_ref, ya_ref, gl_ref, ga_ref, g1_ref, sh2_ref, sc2_ref, gf_ref,
                  wl_ref, wa_ref, wo_ref, wq_ref, x1_ref, h2_ref, qp_ref):
    yl = jnp.dot(yl_ref[...].astype(BF16), wl_ref[...], preferred_element_type=F32)
    ya = jnp.dot(ya_ref[...].astype(BF16), wa_ref[...], preferred_element_type=F32)
    merged = jax.nn.sigmoid(gl_ref[...]) * yl + jax.nn.sigmoid(ga_ref[...]) * ya
    x1 = x_ref[...] + g1_ref[...] * jnp.dot(merged.astype(BF16), wo_ref[...], preferred_element_type=F32)
    x1_ref[...] = x1
    h2 = _modulated_norm(x1, gf_ref[...], sc2_ref[...], sh2_ref[...])
    h2_ref[...] = h2
    qp_ref[...] = jnp.dot(h2.astype(BF16), wq_ref[...], preferred_element_type=F32)


def _outproj(x, yl, ya, gl, ga, gate1, shift2, scale2, g_ffn, wl_bf, wa_bf, wo_bf, wq_bf, per_seq, rows_per_seq):
    n = x.shape[0]
    tile = min(ROW_TILE, n)
    dq = wq_bf.shape[1]
    row = pl.BlockSpec((tile, D_MODEL), lambda i: (i, 0))
    mod = _mod_spec(per_seq, tile, rows_per_seq)
    return pl.pallas_call(
        _outproj_body,
        grid=(n // tile,),
        in_specs=[row] * 5 + [mod] * 3 + [_resident((1, D_MODEL)), _resident(wl_bf.shape), _resident(wa_bf.shape),
                                          _resident(wo_bf.shape), _resident(wq_bf.shape)],
        out_specs=[row, row, pl.BlockSpec((tile, dq), lambda i: (i, 0))],
        out_shape=[jax.ShapeDtypeStruct((n, D_MODEL), F32), jax.ShapeDtypeStruct((n, D_MODEL), F32),
                   jax.ShapeDtypeStruct((n, dq), F32)],
        compiler_params=_params("arbitrary"),
        name="out_proj",
    )(x, yl, ya, gl, ga, gate1, shift2, scale2, g_ffn, wl_bf, wa_bf, wo_bf, wq_bf)


def _colmax(x):
    return jnp.max(x, axis=0, keepdims=True)


def _colmin(x):
    return jnp.min(x, axis=0, keepdims=True)


def _top16_keys(s):
    t = s.shape[1]
    key = lax.broadcasted_iota(I32, s.shape, 0)
    slot = lax.broadcasted_iota(I32, (PEER_TOPK, t), 0)
    vals = jnp.zeros((PEER_TOPK, t), F32)
    idxs = jnp.zeros((PEER_TOPK, t), I32)
    for r in range(PEER_TOPK):
        m = _colmax(s)
        idx = _colmin(jnp.where(s == m, key, N_KEYS))
        vals = jnp.where(slot == r, m, vals)
        idxs = jnp.where(slot == r, idx, idxs)
        s = jnp.where(key == idx, NEG_INF, s)
    return vals, idxs


def _candidates(v1, i1, v2, i2):
    t = v1.shape[1]
    r8 = lax.broadcasted_iota(I32, (SUBLANES, t), 0)
    r16 = lax.broadcasted_iota(I32, (2 * SUBLANES, t), 0)
    lo, hi = slice(0, SUBLANES), slice(SUBLANES, 2 * SUBLANES)
    row = lambda x, a: x[a:a + 1, :]
    vals, eids, flats = [], [], []

    def add(v, e, f, keep=None):
        vals.append(v if keep is None else jnp.where(keep, v, NEG_INF))
        eids.append(e)
        flats.append(f)

    add(row(v1, 0) + v2, row(i1, 0) * N_KEYS + i2, r16)
    add(row(v1, 1) + v2[lo], row(i1, 1) * N_KEYS + i2[lo], PEER_TOPK + r8)
    for a in (2, 3):
        add(row(v1, a) + v2[lo], row(i1, a) * N_KEYS + i2[lo], a * PEER_TOPK + r8)
    add(v1[hi] + row(v2, 0), i1[hi] * N_KEYS + row(i2, 0), (r8 + SUBLANES) * PEER_TOPK)
    for b in (0, 1, 2):
        add(v1[lo] + row(v2, b), i1[lo] * N_KEYS + row(i2, b), r8 * PEER_TOPK + b, keep=r8 >= 4)
    return (jnp.concatenate(vals, axis=0), jnp.concatenate(eids, axis=0), jnp.concatenate(flats, axis=0))


def _top16_candidates(cand, eid, flat):
    t = cand.shape[1]
    slot = lax.broadcasted_iota(I32, (PEER_TOPK, t), 0)
    best = jnp.zeros((PEER_TOPK, t), F32)
    sel = jnp.zeros((PEER_TOPK, t), I32)
    for r in range(PEER_TOPK):
        m = _colmax(cand)
        f = _colmin(jnp.where(cand == m, flat, PEER_TOPK * PEER_TOPK))
        hit = flat == f
        e = _colmax(jnp.where(hit, eid, -1))
        best = jnp.where(slot == r, m, best)
        sel = jnp.where(slot == r, e, sel)
        cand = jnp.where(hit, NEG_INF, cand)
    return best, sel


def _topk_body(q_ref, k1_ref, k2_ref, addr_ref, shift_ref, gate_ref, e_t, g_t):
    def head(h, carry):
        c0 = pl.multiple_of(h * D_QUERY, D_QUERY)
        q1 = q_ref[:, pl.ds(c0, D_HALF)].astype(BF16)
        q2 = q_ref[:, pl.ds(pl.multiple_of(c0 + D_HALF, D_HALF), D_HALF)].astype(BF16)
        s1 = lax.dot_general(k1_ref[h], q1, _NT, preferred_element_type=F32)
        s2 = lax.dot_general(k2_ref[h], q2, _NT, preferred_element_type=F32)
        v1, i1 = _top16_keys(s1)
        v2, i2 = _top16_keys(s2)
        best, sel = _top16_candidates(*_candidates(v1, i1, v2, i2))
        ex = jnp.exp(best - best[0:1, :])
        r0 = pl.multiple_of(h * PEER_TOPK, PEER_TOPK)
        e_t[pl.ds(r0, PEER_TOPK), :] = sel
        g_t[pl.ds(r0, PEER_TOPK), :] = ex / jnp.sum(ex, axis=0, keepdims=True)
        return carry

    def head_group(i, carry):
        for j in range(TOPK_HEADS_PER_ITER):
            head(TOPK_HEADS_PER_ITER * i + j, carry)
        return carry

    lax.fori_loop(0, PEER_HEADS // TOPK_HEADS_PER_ITER, head_group, 0)
    e = e_t[...].T
    addr_ref[...] = (e & (HALF_EXPERTS - 1)) * N_CHUNK
    shift_ref[...] = ((e >> 13) << 4).astype(F32)
    gate_ref[...] = g_t[...].T


def _topk(qp, k1_bf, k2_bf):
    n = qp.shape[0]
    tile = min(TOPK_TILE, n)
    out = pl.BlockSpec((tile, N_PAIRS), lambda i: (i, 0))
    return pl.pallas_call(
        _topk_body,
        grid=(n // tile,),
        in_specs=[pl.BlockSpec((tile, qp.shape[1]), lambda i: (i, 0)), _resident(k1_bf.shape), _resident(k2_bf.shape)],
        out_specs=[out, out, out],
        out_shape=[jax.ShapeDtypeStruct((n, N_PAIRS), I32), jax.ShapeDtypeStruct((n, N_PAIRS), F32),
                   jax.ShapeDtypeStruct((n, N_PAIRS), F32)],
        scratch_shapes=[pltpu.VMEM((N_PAIRS, tile), I32), pltpu.VMEM((N_PAIRS, tile), F32)],
        compiler_params=_params("arbitrary"),
        name="peer_topk",
    )(qp, k1_bf, k2_bf)


def _pack_table(t):
    b = lax.bitcast_convert_type(t.astype(BF16), jnp.uint16).astype(jnp.uint32)
    w = (b[:HALF_EXPERTS] << 16) | b[HALF_EXPERTS:]
    return w.reshape(HALF_EXPERTS * N_CHUNK, LANES)


def _expert_row(tab_ref, addr, shift_ref, k):
    w = tab_ref[pl.ds(pl.multiple_of(addr, N_CHUNK), N_CHUNK), :]
    sh = jnp.broadcast_to(shift_ref[pl.ds(k, 1), :], (N_CHUNK, LANES))
    return lax.bitcast_convert_type((w << lax.bitcast_convert_type(sh, jnp.uint32)) & jnp.uint32(0xFFFF0000), F32)


def _butterfly(parts, index, steps, axis):
    for step in steps:
        m = (index & step) == 0
        parts = [jnp.where(m, a, b) + pltpu.roll(jnp.where(m, b, a), step, axis)
                 for a, b in zip(parts[0::2], parts[1::2])]
    return parts[0]


def _lane_selector(tile):
    k = np.arange(2 * LANES)[:, None] % LANES
    t = np.arange(tile * LANES)[None, :] // LANES
    return jnp.asarray(k == t, dtype=BF16)


def _rows_from_lanes(x, sel_ref, rows_ref, exact_bf16, dtype):
    t = x.shape[0]
    xt = jnp.concatenate([x, jnp.zeros((LANES - t, LANES), F32)], axis=0).T
    hi = xt.astype(BF16)
    if exact_bf16:
        out = jnp.dot(hi, sel_ref[0:LANES, :], preferred_element_type=F32)
    else:
        lo = (xt - hi.astype(F32)).astype(BF16)
        out = jnp.dot(jnp.concatenate([hi, lo], axis=1), sel_ref[...], preferred_element_type=F32)
    for i in range(t):
        rows_ref[i * LANES:(i + 1) * LANES, :] = out[:, i * LANES:(i + 1) * LANES].astype(dtype)


def _row_sums_to_lanes(s, tile):
    hi = s.astype(BF16)
    lo = (s - hi.astype(F32)).astype(BF16)
    r = jnp.dot(jnp.concatenate([hi, lo], axis=1), jnp.ones((2 * LANES, LANES), BF16), preferred_element_type=F32)
    groups = N_PAIRS // SUBLANES
    r = r.reshape(tile, groups, SUBLANES, LANES)
    shape = (groups, SUBLANES, LANES)
    lane = lax.broadcasted_iota(I32, shape, 2)
    keep = ((lane >> 3) == lax.broadcasted_iota(I32, shape, 0)) & ((lane & 7) == lax.broadcasted_iota(I32, shape, 1))
    return jnp.sum(jnp.where(keep[None], r, 0.0), axis=(1, 2))


def _peer_u_body(addr_ref, shift_ref, h_ref, gate_ref, sel_ref, tab_ref, coef_ref, sh_rows, part_s, *, tile):
    _rows_from_lanes(shift_ref[...], sel_ref, sh_rows, True, I32)
    sub = lax.broadcasted_iota(I32, (SUBLANES, LANES), 0)

    def token(t, carry):
        h = h_ref[pl.ds(pl.multiple_of(t * N_CHUNK, N_CHUNK), N_CHUNK), :]
        for g in range(N_PAIRS // SUBLANES):
            prods = []
            for r in range(SUBLANES):
                k = t * N_PAIRS + g * SUBLANES + r
                prods.append(_expert_row(tab_ref, addr_ref[0, 0, k], sh_rows, k) * h)
            row0 = pl.multiple_of(t * N_PAIRS + g * SUBLANES, SUBLANES)
            part_s[pl.ds(row0, SUBLANES), :] = _butterfly(prods, sub, (1, 2, 4), 0)
        return carry

    lax.fori_loop(0, tile, token, 0)
    coef_ref[...] = gate_ref[...] * jax.nn.gelu(_row_sums_to_lanes(part_s[...], tile))


def _peer_v_body(addr_ref, shift_ref, coef_ref, sel_ref, tab_ref, o_ref, sh_rows, coef_rows, *, tile):
    _rows_from_lanes(shift_ref[...], sel_ref, sh_rows, True, I32)
    _rows_from_lanes(coef_ref[...], sel_ref, coef_rows, False, F32)

    def token(t, carry):
        accs = [jnp.zeros((N_CHUNK, LANES), F32) for _ in range(4)]
        for j in range(N_PAIRS):
            k = t * N_PAIRS + j
            c = jnp.broadcast_to(coef_rows[pl.ds(k, 1), :], (N_CHUNK, LANES))
            accs[j % 4] = accs[j % 4] + c * _expert_row(tab_ref, addr_ref[0, 0, k], sh_rows, k)
        o_ref[pl.ds(pl.multiple_of(t * N_CHUNK, N_CHUNK), N_CHUNK), :] = (accs[0] + accs[1]) + (accs[2] + accs[3])
        return carry

    lax.fori_loop(0, tile, token, 0)


def _peer_apply(h2, addr, shift, gate, tab_u, tab_v):
    n = h2.shape[0]
    tile = min(PEER_TILE, n)
    steps = n // tile
    smem = pl.BlockSpec((1, 1, tile * N_PAIRS), lambda i: (i, 0, 0), memory_space=pltpu.SMEM)
    rows = pl.BlockSpec((tile * N_CHUNK, LANES), lambda i: (i, 0))
    pairs = pl.BlockSpec((tile, N_PAIRS), lambda i: (i, 0))
    addr_s = addr.reshape(steps, 1, tile * N_PAIRS)
    rep = lambda dtype: pltpu.VMEM((tile * N_PAIRS, LANES), dtype)
    sel = _lane_selector(tile)
    coef = pl.pallas_call(
        functools.partial(_peer_u_body, tile=tile),
        grid=(steps,),
        in_specs=[smem, pairs, rows, pairs, _resident(sel.shape), _resident(tab_u.shape)],
        out_specs=pairs,
        out_shape=jax.ShapeDtypeStruct((n, N_PAIRS), F32),
        scratch_shapes=[rep(I32), rep(F32)],
        compiler_params=_params("arbitrary"),
        name="peer_score",
    )(addr_s, shift, h2.reshape(n * N_CHUNK, LANES), gate, sel, tab_u)
    out = pl.pallas_call(
        functools.partial(_peer_v_body, tile=tile),
        grid=(steps,),
        in_specs=[smem, pairs, pairs, _resident(sel.shape), _resident(tab_v.shape)],
        out_specs=rows,
        out_shape=jax.ShapeDtypeStruct((n * N_CHUNK, LANES), F32),
        scratch_shapes=[rep(I32), rep(F32)],
        compiler_params=_params("arbitrary"),
        name="peer_mix",
    )(addr_s, shift, coef, sel, tab_v)
    return out.reshape(n, D_MODEL)


def _final_body(x_ref, p_ref, g2_ref, gf_ref, y_ref):
    x = x_ref[...] + g2_ref[...] * p_ref[...]
    ms = jnp.mean(x * x, axis=-1, keepdims=True)
    y_ref[...] = x * lax.rsqrt(ms + EPS) * gf_ref[...]


def _final(x1, peer_out, gate2, g_final, per_seq, rows_per_seq):
    n = x1.shape[0]
    tile = min(ROW_TILE, n)
    row = pl.BlockSpec((tile, D_MODEL), lambda i: (i, 0))
    return pl.pallas_call(
        _final_body,
        grid=(n // tile,),
        in_specs=[row, row, _mod_spec(per_seq, tile, rows_per_seq), _resident((1, D_MODEL))],
        out_specs=row,
        out_shape=jax.ShapeDtypeStruct((n, D_MODEL), F32),
        compiler_params=_params("arbitrary"),
        name="final_norm",
    )(x1, peer_out, gate2, g_final)


def _group_layer(x, mod, per_seq, attend, conv_prev, h0, p):
    nseq, t, _ = x.shape
    n = nseq * t
    xf = x.reshape(n, D_MODEL)
    if per_seq:
        mods = [m.reshape(nseq, 1, D_MODEL) for m in jnp.split(mod, 6, axis=-1)]
    else:
        mods = [jnp.repeat(m, t, axis=0) for m in jnp.split(mod, 6, axis=-1)]
    shift1, scale1, gate1, shift2, scale2, gate2 = mods

    xl, q, k, v, gl, ga = _inproj(xf, shift1, scale1, p["g_mix"], p["w_in"], per_seq, t)

    t_pad = -(-t // SUBLANES) * SUBLANES
    xl3 = jnp.pad(xl.reshape(nseq, t, D_MODEL), ((0, 0), (0, t_pad - t), (0, 0)))
    cprev8 = jnp.pad(conv_prev, ((0, 0), (SUBLANES - (CONV_W - 1), 0), (0, 0)))
    yl, conv8, h_last = _lru(xl3, cprev8, h0.reshape(nseq, 1, D_MODEL), p["conv_w"], p["conv_b"],
                             p["lru_wa"], p["lru_wx"], p["lru_ba"], p["lru_bx"], p["lru_lambda"], t)
    yl = yl[:, :t].reshape(n, D_MODEL)
    conv_new = conv8[:, SUBLANES - (CONV_W - 1):]
    h_new = h_last.reshape(nseq, D_MODEL)

    ya = attend(q, k, v).reshape(n, D_MODEL)

    x1, h2, qp = _outproj(xf, yl, ya, gl, ga, gate1, shift2, scale2, p["g_ffn"], p["w_lru_out"], p["w_att_out"],
                          p["w_o"], p["peer_wq"], per_seq, t)
    addr, shift, gate = _topk(qp, p["peer_k1"], p["peer_k2"])
    peer_out = _peer_apply(h2, addr, shift, gate, p["peer_u"], p["peer_v"])
    y = _final(x1, peer_out, gate2, p["g_final"], per_seq, t)
    kv_shape = (nseq, t, N_HEADS, HEAD_DIM)
    return y.reshape(nseq, t, D_MODEL), k.reshape(kv_shape), v.reshape(kv_shape), conv_new, h_new


def kernel(x_prompt, x_sample, cache_k, cache_v, state_conv, state_lru, page_table, c_prompt, c_sample,
           g_mix, w_ada, b_ada, w_in, conv_w, conv_b, lru_wa, lru_ba, lru_wx, lru_bx, lru_lambda, sb_bias,
           w_lru_out, w_att_out, w_o, g_ffn, peer_wq, peer_k1, peer_k2, peer_u, peer_v, g_final):
    depth = w_in.shape[0]
    nb = x_prompt.shape[0]
    nd, t_dec, _ = x_sample.shape
    n_phys = cache_k.shape[1]
    row = lambda a: a.reshape(1, D_MODEL)
    xp, xs = x_prompt, x_sample
    outs = [[] for _ in range(8)]
    for l in range(depth):
        p = {"g_mix": row(g_mix[l]), "w_in": w_in[l].astype(BF16), "conv_w": conv_w[l], "conv_b": row(conv_b[l]),
             "lru_wa": lru_wa[l].astype(BF16), "lru_wx": lru_wx[l].astype(BF16), "lru_ba": row(lru_ba[l]),
             "lru_bx": row(lru_bx[l]), "lru_lambda": row(lru_lambda[l]), "w_lru_out": w_lru_out[l].astype(BF16),
             "w_att_out": w_att_out[l].astype(BF16), "w_o": w_o[l].astype(BF16), "g_ffn": row(g_ffn[l]),
             "peer_wq": peer_wq[l].astype(BF16), "peer_k1": peer_k1[l].astype(BF16),
             "peer_k2": peer_k2[l].astype(BF16), "peer_u": _pack_table(peer_u[l]), "peer_v": _pack_table(peer_v[l]),
             "g_final": row(g_final)}
        bias = sb_bias[l] * LOG2E
        mod = _ada(jnp.concatenate([c_prompt, c_sample], axis=0), w_ada[l].astype(BF16), b_ada[l].reshape(1, -1))

        def attend_prompt(q, k, v):
            shape = (nb, q.shape[0] // nb, D_MODEL)
            return _attn_prompt(q.reshape(shape), k.reshape(shape), v.reshape(shape), bias)

        def attend_sample(q, k, v, l=l):
            q8 = jnp.pad(q.reshape(nd, t_dec, D_MODEL), ((0, 0), (0, SUBLANES - t_dec), (0, 0)))
            paged = lambda c: c[l].reshape(n_phys, PAGE_SIZE * N_HEADS, HEAD_DIM)
            new = lambda a: a.reshape(nd, t_dec * N_HEADS, HEAD_DIM)
            out = _attn_sample(q8, new(k), new(v), bias, paged(cache_k), paged(cache_v), page_table)
            return out[:, :t_dec]

        zeros_conv = jnp.zeros((nb, CONV_W - 1, D_MODEL), F32)
        zeros_h = jnp.zeros((nb, D_MODEL), F32)
        xp, kp, vp, cp, hp = _group_layer(xp, mod[:nb], True, attend_prompt, zeros_conv, zeros_h, p)
        xs, ks, vs, cs, hs = _group_layer(xs, mod[nb:], False, attend_sample, state_conv[l], state_lru[l], p)
        for lst, val in zip(outs, (kp, vp, cp, hp, ks, vs, cs, hs)):
            lst.append(val)
    assert depth == 1
    return (xp, xs) + tuple(jnp.stack(o) for o in outs)
```
